```python
import numpy as np
import jax
import jax.numpy as jnp
from jax import lax

D_MODEL = 1024
BATCH = 8
SEQ = 8192
DEPTH = 2
DEC_BATCH = 8
DEC_SEQ = 16
PAST_LEN = 2048

CHUNK = 64
RET_HEADS = 4
RET_DK = 128
RET_DV = 256
RET_QK = RET_HEADS * RET_DK
RET_V = RET_HEADS * RET_DV
ROPE_BASE = 10000.0
HG_HEADS = 8
HG_DK = 128
HG_DV = D_MODEL // HG_HEADS
HG_K = HG_HEADS * HG_DK
HG_V = HG_HEADS * HG_DV
HG_MIN_F = 1e-6
RG_BLOCKS = 5
RG_BLOCK = 256
RG_WIDTH = RG_BLOCKS * RG_BLOCK
RG_CONV = 4
RG_C = 8.0
D_FF = 2816
FFN_CONV = 3
MIX_WIDTH = RET_V + HG_V + RG_WIDTH
IN_SPLITS = (RET_QK, RET_QK, RET_V, RET_V, HG_K, HG_K, HG_V, HG_V, RG_WIDTH, RG_WIDTH, D_MODEL, D_MODEL, D_MODEL)
IN_WIDTH = sum(IN_SPLITS)
EPS = 1e-6

kernel_name = 'hybrid_retention_hgrn2_rglru_streaming_step'


def _rmsnorm(x, w):
    xf = x.astype(jnp.float32)
    y = xf * lax.rsqrt(jnp.mean(xf * xf, axis=-1, keepdims=True) + EPS)
    return (y * w.astype(jnp.float32)).astype(x.dtype)


def _group_rms(o):
    return o * lax.rsqrt(jnp.mean(o * o, axis=-1, keepdims=True) + EPS)


def _split(a, sizes):
    idx = [int(i) for i in np.cumsum(sizes)[:-1]]
    return jnp.split(a, idx, axis=-1)


def _heads(a, h):
    b, t, _ = a.shape
    return a.reshape(b, t, h, -1).transpose(0, 2, 1, 3)


def _merge_heads(a):
    b, h, t, d = a.shape
    return a.transpose(0, 2, 1, 3).reshape(b, t, h * d)


def _rotary(x, pos0):
    t, d = x.shape[2], x.shape[3]
    half = d // 2
    inv = jnp.power(ROPE_BASE, -jnp.arange(half, dtype=jnp.float32) / half)
    ang = (jnp.arange(t, dtype=jnp.float32) + pos0)[:, None] * inv[None, :]
    cos, sin = jnp.cos(ang), jnp.sin(ang)
    x1, x2 = x[..., :half], x[..., half:]
    return jnp.concatenate([x1 * cos - x2 * sin, x1 * sin + x2 * cos], axis=-1)


def _causal_dwconv(u, buf, w, b):
    k = w.shape[0]
    t = u.shape[1]
    up = jnp.concatenate([buf.astype(u.dtype), u], axis=1)
    y = b
    for j in range(k):
        y = y + w[j] * up[:, j:j + t]
    return y, up[:, t:]


def _chunked(step, s0, seqs):
    b, h, t = seqs[0].shape[:3]
    if t <= CHUNK:
        return step(s0, seqs)
    nc = t // CHUNK
    xs = tuple(jnp.moveaxis(a.reshape(b, h, nc, CHUNK, a.shape[-1]), 2, 0) for a in seqs)
    s, o = lax.scan(step, s0, xs)
    return s, jnp.moveaxis(o, 0, 2).reshape(b, h, t, o.shape[-1])


def _retention(q, k, v, g, r0, pos0):
    f32 = jnp.float32
    q = _rotary(_heads(q, RET_HEADS).astype(f32), pos0)
    k = _rotary(_heads(k, RET_HEADS).astype(f32), pos0) * (RET_DK ** -0.5)
    v = _heads(v, RET_HEADS).astype(f32)
    log_gamma = jnp.log1p(-jnp.power(2.0, -5.0 - jnp.arange(RET_HEADS, dtype=f32)))
    lg = log_gamma[:, None]

    def step(r, xs):
        qc, kc, vc = xs
        n_len = qc.shape[2]
        n = jnp.arange(n_len, dtype=f32)
        intra = jnp.exp(jnp.abs(n[:, None] - n[None, :])[None] * lg[:, :, None])
        scores = jnp.einsum('bhnk,bhmk->bhnm', qc, kc) * intra
        q_dec = jnp.exp(lg * (n + 1.0))[:, :, None]
        k_dec = jnp.exp(lg * (n_len - 1.0 - n))[:, :, None]
        o = jnp.einsum('bhnm,bhmv->bhnv', scores, vc) + jnp.einsum('bhnk,bhkv->bhnv', qc * q_dec, r)
        r_new = jnp.exp(lg * n_len)[:, :, None] * r + jnp.einsum('bhmk,bhmv->bhkv', kc * k_dec, vc)
        return r_new, o

    r, o = _chunked(step, r0.astype(f32), (q, k, v))
    o = _merge_heads(_group_rms(o))
    return o * jax.nn.silu(g.astype(f32)), r


def _hgrn2(q, f, i, g, s0, lb, norm_w):
    f32 = jnp.float32
    q = jax.nn.silu(_heads(q, HG_HEADS).astype(f32))
    z = _heads(f, HG_HEADS).astype(f32)
    lbh = lb.astype(f32).reshape(HG_HEADS, 1, HG_DK)
    fgate = lbh + (1.0 - lbh) * jax.nn.sigmoid(z)
    logf = jnp.log(jnp.maximum(fgate, HG_MIN_F))
    k = (1.0 - lbh) * jax.nn.sigmoid(-z)
    v = _heads(i, HG_HEADS).astype(f32)

    def step(s, xs):
        qc, kc, vc, gc = xs
        n_len = qc.shape[2]
        bcum = jnp.cumsum(gc, axis=2)
        causal = jnp.tril(jnp.ones((n_len, n_len), bool))[:, :, None]
        diff = bcum[:, :, :, None, :] - bcum[:, :, None, :, :]
        decay = jnp.where(causal, jnp.exp(jnp.where(causal, diff, 0.0)), 0.0)
        attn = jnp.einsum('bhtk,bhsk,bhtsk->bhts', qc, kc, decay)
        o = jnp.einsum('bhts,bhsv->bhtv', attn, vc) + jnp.einsum('bhtk,bhkv->bhtv', qc * jnp.exp(bcum), s)
        b_last = bcum[:, :, -1:, :]
        s_new = jnp.exp(b_last[:, :, 0, :, None]) * s + jnp.einsum('bhsk,bhsv->bhkv', kc * jnp.exp(b_last - bcum), vc)
        return s_new, o

    s, o = _chunked(step, s0.astype(f32), (q, k, v, logf))
    o = _merge_heads(_group_rms(o)) * norm_w.astype(f32)
    return o * jax.nn.silu(g.astype(f32)), s


def _lin_combine(c1, c2):
    a1, b1 = c1
    a2, b2 = c2
    return a1 * a2, a2 * b1 + b2


def _rglru(u, y, h0, buf, conv_w, conv_b, w_r, b_r, w_i, b_i, lam, pos0):
    f32 = jnp.float32
    b, t, _ = u.shape
    xc, new_buf = _causal_dwconv(u, buf, conv_w, conv_b)
    xc = xc.astype(f32)
    xb = xc.reshape(b, t, RG_BLOCKS, RG_BLOCK)
    r = jax.nn.sigmoid(jnp.einsum('btnd,nde->btne', xb, w_r.astype(f32)).reshape(b, t, RG_WIDTH) + b_r.astype(f32))
    ig = jax.nn.sigmoid(jnp.einsum('btnd,nde->btne', xb, w_i.astype(f32)).reshape(b, t, RG_WIDTH) + b_i.astype(f32))
    log_a = -RG_C * r * jax.nn.softplus(-lam.astype(f32))
    a = jnp.exp(log_a)
    mult = jnp.sqrt(jnp.maximum(-jnp.expm1(2.0 * log_a), 0.0))
    pos = jnp.arange(t) + pos0
    mult = jnp.where((pos == 0)[None, :, None], 1.0, mult)
    bt = mult * (ig * xc)
    bt = bt.at[:, 0].add(a[:, 0] * h0.astype(f32))
    _, h = lax.associative_scan(_lin_combine, (a, bt), axis=1)
    return h * jax.nn.gelu(y.astype(f32)), h[:, -1], new_buf


def _layer(x, pos0, r0, s0, h0, rgb0, ffb0, lb, norm1_w, w_in, w_branch, w_out, rg_conv_w, rg_conv_b,
           rg_w_r, rg_b_r, rg_w_i, rg_b_i, rg_lambda, hg_norm_w, norm2_w, w_up, ffn_conv_w, ffn_conv_b, w_down):
    f32 = jnp.float32
    hn = _rmsnorm(x, norm1_w)
    proj = hn @ w_in
    rq, rk, rv, rgate, hq, hf, hi, hgate, ru, ry, g_ret, g_hg, g_rg = _split(proj, IN_SPLITS)
    o_ret, r_new = _retention(rq, rk, rv, rgate, r0, pos0)
    o_hg, s_new = _hgrn2(hq, hf, hi, hgate, s0, lb, hg_norm_w)
    o_rg, h_new, rgb_new = _rglru(ru, ry, h0, rgb0, rg_conv_w, rg_conv_b, rg_w_r, rg_b_r, rg_w_i, rg_b_i, rg_lambda, pos0)
    wb_ret, wb_hg, wb_rg = jnp.split(w_branch, [RET_V, RET_V + HG_V], axis=0)
    mixed = (jax.nn.sigmoid(g_ret.astype(f32)) * (o_ret @ wb_ret)
             + jax.nn.sigmoid(g_hg.astype(f32)) * (o_hg @ wb_hg)
             + jax.nn.sigmoid(g_rg.astype(f32)) * (o_rg @ wb_rg))
    x = x + mixed.astype(x.dtype) @ w_out
    hn = _rmsnorm(x, norm2_w)
    a, gate = jnp.split(hn @ w_up, [D_FF], axis=-1)
    a, ffb_new = _causal_dwconv(a, ffb0, ffn_conv_w, ffn_conv_b)
    x = x + (jax.nn.gelu(a) * gate) @ w_down
    return x, r_new, s_new, h_new, rgb_new, ffb_new


def _trunk(x, pos0, r0, s0, h0, rgb0, ffb0, lbs, weights, final_norm_w):
    per_layer = []
    for l in range(DEPTH):
        x, r, s, h, rgb, ffb = _layer(x, pos0, r0[l], s0[l], h0[l], rgb0[l], ffb0[l], lbs[l],
                                      *[w[l] for w in weights])
        per_layer.append((r, s, h, rgb, ffb))
    new_states = tuple(jnp.stack(st, axis=0) for st in zip(*per_layer))
    return _rmsnorm(x, final_norm_w), new_states


def setup_inputs(seed: int = 0) -> dict:
    key = jax.random.key(seed)
    ks = jax.random.split(key, 32)
    f32 = jnp.float32

    def nrm(k, shape, scale):
        return jax.random.normal(k, shape, f32) * scale

    a8 = jax.random.uniform(ks[0], (DEPTH, RG_WIDTH), f32, 0.9, 0.999)
    a1 = a8 ** (1.0 / RG_C)
    rg_lambda = jnp.log(a1) - jnp.log1p(-a1)
    return {
        'x_prompt': nrm(ks[1], (BATCH, SEQ, D_MODEL), 1.0),
        'x_sample': nrm(ks[2], (DEC_BATCH, DEC_SEQ, D_MODEL), 1.0),
        'state_ret': nrm(ks[3], (DEPTH, DEC_BATCH, RET_HEADS, RET_DK, RET_DV), 0.5),
        'state_hgrn': nrm(ks[4], (DEPTH, DEC_BATCH, HG_HEADS, HG_DK, HG_DV), 0.5),
        'state_rglru': nrm(ks[5], (DEPTH, DEC_BATCH, RG_WIDTH), 0.5),
        'cache_rg_conv': nrm(ks[6], (DEPTH, DEC_BATCH, RG_CONV - 1, RG_WIDTH), 1.0),
        'cache_ffn_conv': nrm(ks[7], (DEPTH, DEC_BATCH, FFN_CONV - 1, D_FF), 1.0),
        'norm1_w': 1.0 + nrm(ks[8], (DEPTH, D_MODEL), 0.01),
        'w_in': nrm(ks[9], (DEPTH, D_MODEL, IN_WIDTH), D_MODEL ** -0.5),
        'w_branch': nrm(ks[10], (DEPTH, MIX_WIDTH, D_MODEL), D_MODEL ** -0.5),
        'w_out': nrm(ks[11], (DEPTH, D_MODEL, D_MODEL), D_MODEL ** -0.5),
        'rg_conv_w': nrm(ks[12], (DEPTH, RG_CONV, RG_WIDTH), RG_CONV ** -0.5),
        'rg_conv_b': nrm(ks[13], (DEPTH, RG_WIDTH), 0.01),
        'rg_w_r': nrm(ks[14], (DEPTH, RG_BLOCKS, RG_BLOCK, RG_BLOCK), RG_BLOCK ** -0.5),
        'rg_b_r': nrm(ks[15], (DEPTH, RG_WIDTH), 0.01),
        'rg_w_i': nrm(ks[16], (DEPTH, RG_BLOCKS, RG_BLOCK, RG_BLOCK), RG_BLOCK ** -0.5),
        'rg_b_i': nrm(ks[17], (DEPTH, RG_WIDTH), 0.01),
        'rg_lambda': rg_lambda,
        'hg_lb': nrm(ks[18], (DEPTH, HG_K), 1.0),
        'hg_norm_w': 1.0 + nrm(ks[19], (DEPTH, HG_V), 0.01),
        'norm2_w': 1.0 + nrm(ks[20], (DEPTH, D_MODEL), 0.01),
        'w_up': nrm(ks[21], (DEPTH, D_MODEL, 2 * D_FF), D_MODEL ** -0.5),
        'ffn_conv_w': nrm(ks[22], (DEPTH, FFN_CONV, D_FF), FFN_CONV ** -0.5),
        'ffn_conv_b': nrm(ks[23], (DEPTH, D_FF), 0.01),
        'w_down': nrm(ks[24], (DEPTH, D_FF, D_MODEL), D_FF ** -0.5),
        'final_norm_w': 1.0 + nrm(ks[25], (D_MODEL,), 0.01),
    }


def reference(x_prompt, x_sample, state_ret, state_hgrn, state_rglru, cache_rg_conv, cache_ffn_conv,
              norm1_w, w_in, w_branch, w_out, rg_conv_w, rg_conv_b, rg_w_r, rg_b_r, rg_w_i, rg_b_i,
              rg_lambda, hg_lb, hg_norm_w, norm2_w, w_up, ffn_conv_w, ffn_conv_b, w_down, final_norm_w):
    f32 = jnp.float32
    sm = jax.nn.softmax(hg_lb.astype(f32), axis=0)
    lbs = jnp.cumsum(sm, axis=0) - sm[:1]
    weights = (norm1_w, w_in, w_branch, w_out, rg_conv_w, rg_conv_b, rg_w_r, rg_b_r, rg_w_i, rg_b_i,
               rg_lambda, hg_norm_w, norm2_w, w_up, ffn_conv_w, ffn_conv_b, w_down)

    y_prompt, (ret_p, hg_p, rgh_p, rgc_p, ffc_p) = _trunk(
        x_prompt, 0,
        jnp.zeros((DEPTH, BATCH, RET_HEADS, RET_DK, RET_DV), f32),
        jnp.zeros((DEPTH, BATCH, HG_HEADS, HG_DK, HG_DV), f32),
        jnp.zeros((DEPTH, BATCH, RG_WIDTH), f32),
        jnp.zeros((DEPTH, BATCH, RG_CONV - 1, RG_WIDTH), f32),
        jnp.zeros((DEPTH, BATCH, FFN_CONV - 1, D_FF), f32),
        lbs, weights, final_norm_w)

    y_sample, (ret_s, hg_s, rgh_s, rgc_s, ffc_s) = _trunk(
        x_sample, PAST_LEN, state_ret, state_hgrn, state_rglru, cache_rg_conv, cache_ffn_conv,
        lbs, weights, final_norm_w)

    return (y_prompt, y_sample, ret_p, ret_s, hg_p, hg_s, rgh_p, rgh_s, rgc_p, rgc_s, ffc_p, ffc_s)
```

```python
import functools
import math

import numpy as np
import jax
import jax.numpy as jnp
from jax import lax
from jax.experimental import pallas as pl
from jax.experimental.pallas import tpu as pltpu

F32 = jnp.float32
BF16 = jnp.bfloat16

D_MODEL = 1024
PAST_LEN = 2048
CHUNK = 64
RET_HEADS, RET_DK, RET_DV = 4, 128, 256
RET_QK = RET_HEADS * RET_DK
RET_V = RET_HEADS * RET_DV
ROPE_BASE = 10000.0
HG_HEADS, HG_DK, HG_DV = 8, 128, 128
HG_K = HG_HEADS * HG_DK
HG_V = HG_HEADS * HG_DV
HG_MIN_F = 1e-6
RG_BLOCKS, RG_BLOCK = 5, 256
RG_WIDTH = RG_BLOCKS * RG_BLOCK
RG_CONV = 4
RG_C = 8.0
D_FF = 2816
FFN_CONV = 3
EPS = 1e-6

RET_COLS = 2 * RET_QK + 2 * RET_V
HG_COLS = 2 * HG_K + 2 * HG_V
RG_COLS = 2 * RG_WIDTH
GATE_COLS = 3 * D_MODEL
OFF_HG = RET_COLS
OFF_RG = OFF_HG + HG_COLS
OFF_GATE = OFF_RG + RG_COLS

VMEM_LIMIT_BYTES = 56 * 1024 * 1024
SUBLANES = 8
GELU_C = math.sqrt(2.0 / math.pi)

LOG_GAMMA = tuple(math.log1p(-(2.0 ** (-5.0 - h))) for h in range(RET_HEADS))


def _cparams(sem):
    return pltpu.CompilerParams(dimension_semantics=sem, vmem_limit_bytes=VMEM_LIMIT_BYTES)


def _resident(shape):
    nd = len(shape)
    return pl.BlockSpec(shape, lambda *_: (0,) * nd, pipeline_mode=pl.Buffered(1))


def _sigmoid(x):
    return jax.nn.sigmoid(x)


def _silu(x):
    return x * jax.nn.sigmoid(x)


def _gelu(x):
    return 0.5 * x * (1.0 + jnp.tanh(GELU_C * (x + 0.044715 * (x * x * x))))


def _rms(x, w):
    ms = jnp.mean(x * x, axis=-1, keepdims=True)
    return x * lax.rsqrt(ms + EPS) * w


def _dot(a, b):
    return jnp.dot(a, b, preferred_element_type=F32)


def _dot_nt(a, b):
    return lax.dot_general(a, b, (((1,), (1,)), ((), ())), preferred_element_type=F32)


def _dot_tn(a, b):
    return lax.dot_general(a, b, (((0,), (0,)), ((), ())), preferred_element_type=F32)


def _norm_kernel(x_ref, w_ref, o_ref):
    o_ref[...] = _rms(x_ref[...], w_ref[...]).astype(o_ref.dtype)


def _norm(x2d, w, tm):
    rows = x2d.shape[0]
    return pl.pallas_call(
        _norm_kernel,
        grid=(rows // tm,),
        in_specs=[pl.BlockSpec((tm, D_MODEL), lambda i: (i, 0)), _resident((1, D_MODEL))],
        out_specs=pl.BlockSpec((tm, D_MODEL), lambda i: (i, 0)),
        out_shape=jax.ShapeDtypeStruct((rows, D_MODEL), BF16),
        compiler_params=_cparams(("parallel",)),
        name="norm",
    )(x2d, w.reshape(1, D_MODEL))


def _ret_kernel(hn_ref, w_ref, cos_ref, sin_ref, r0_ref, o_ref, r_ref, proj_ref, *, chunk, n_chunks):
    t = pl.program_id(1)

    @pl.when(t == 0)
    def _():
        r_ref[...] = r0_ref[...]

    proj_ref[...] = _dot(hn_ref[0], w_ref[...])

    row = lax.broadcasted_iota(jnp.int32, (chunk, chunk), 0)
    col = lax.broadcasted_iota(jnp.int32, (chunk, chunk), 1)
    dist = jnp.abs(row - col).astype(F32)
    n = lax.broadcasted_iota(jnp.int32, (chunk, RET_DK), 0).astype(F32)

    def body(c, carry):
        rows = pl.ds(pl.multiple_of(c * chunk, chunk), chunk)
        cos = cos_ref[rows, :]
        sin = sin_ref[rows, :]
        for h in range(RET_HEADS):
            lg = LOG_GAMMA[h]
            q = proj_ref[rows, h * RET_DK:(h + 1) * RET_DK]
            k = proj_ref[rows, RET_QK + h * RET_DK:RET_QK + (h + 1) * RET_DK]
            v = proj_ref[rows, 2 * RET_QK + h * RET_DV:2 * RET_QK + (h + 1) * RET_DV]
            g = proj_ref[rows, 2 * RET_QK + RET_V + h * RET_DV:2 * RET_QK + RET_V + (h + 1) * RET_DV]
            q = q * cos + pltpu.roll(q, RET_DK // 2, 1) * sin
            k = (k * cos + pltpu.roll(k, RET_DK // 2, 1) * sin) * (RET_DK ** -0.5)
            vb = v.astype(BF16)
            scores = _dot_nt(q.astype(BF16), k.astype(BF16)) * jnp.exp(dist * lg)
            q_dec = jnp.exp(lg * (n + 1.0))
            k_dec = jnp.exp(lg * (float(chunk) - 1.0 - n))
            r = r_ref[0, h]
            o = _dot(scores.astype(BF16), vb) + _dot((q * q_dec).astype(BF16), r.astype(BF16))
            r_ref[0, h] = math.exp(lg * chunk) * r + _dot_tn((k * k_dec).astype(BF16), vb)
            o = o * lax.rsqrt(jnp.mean(o * o, axis=-1, keepdims=True) + EPS)
            o_ref[0, rows, h * RET_DV:(h + 1) * RET_DV] = (o * _silu(g)).astype(o_ref.dtype)
        return carry

    lax.fori_loop(0, n_chunks, body, 0)


def _retention(hn, w, cos2, sin2, r0, tb, chunk):
    b, t, _ = hn.shape
    nt = t // tb
    return pl.pallas_call(
        functools.partial(_ret_kernel, chunk=chunk, n_chunks=tb // chunk),
        grid=(b, nt),
        in_specs=[
            pl.BlockSpec((1, tb, D_MODEL), lambda i, j: (i, j, 0)),
            _resident((D_MODEL, RET_COLS)),
            pl.BlockSpec((tb, RET_DK), lambda i, j: (j, 0)),
            pl.BlockSpec((tb, RET_DK), lambda i, j: (j, 0)),
            pl.BlockSpec((1, RET_HEADS, RET_DK, RET_DV), lambda i, j: (i, 0, 0, 0)),
        ],
        out_specs=[
            pl.BlockSpec((1, tb, RET_V), lambda i, j: (i, j, 0)),
            pl.BlockSpec((1, RET_HEADS, RET_DK, RET_DV), lambda i, j: (i, 0, 0, 0)),
        ],
        out_shape=[
            jax.ShapeDtypeStruct((b, t, RET_V), BF16),
            jax.ShapeDtypeStruct((b, RET_HEADS, RET_DK, RET_DV), F32),
        ],
        scratch_shapes=[pltpu.VMEM((tb, RET_COLS), F32)],
        compiler_params=_cparams(("parallel", "arbitrary")),
        name="retention",
    )(hn, w, cos2, sin2, r0)


def _hg_kernel(hn_ref, w_ref, lb_ref, nw_ref, s0_ref, o_ref, s_ref, proj_ref, st_ref,
               *, chunk, n_chunks, layer, depth):
    t = pl.program_id(1)

    @pl.when(t == 0)
    def _():
        for h in range(HG_HEADS):
            st_ref[h] = s0_ref[0, h].T

    proj_ref[...] = _dot(hn_ref[0], w_ref[...])

    lrows = [lb_ref[j:j + 1, :] for j in range(depth)]
    mx = functools.reduce(jnp.maximum, lrows)
    ex = [jnp.exp(r - mx) for r in lrows]
    den = functools.reduce(lambda a, c: a + c, ex)
    lb = jnp.zeros_like(mx)
    for j in range(1, layer + 1):
        lb = lb + ex[j] / den
    one_m_lb = 1.0 - lb

    row = lax.broadcasted_iota(jnp.int32, (chunk, chunk), 0)
    col = lax.broadcasted_iota(jnp.int32, (chunk, chunk), 1)
    causal = row >= col
    tri = jnp.where(causal, 1.0, 0.0).astype(BF16)
    nw = nw_ref[...]

    def body(c, carry):
        rows = pl.ds(pl.multiple_of(c * chunk, chunk), chunk)
        z = proj_ref[rows, HG_K:2 * HG_K]
        fg = lb + one_m_lb * _sigmoid(z)
        logf = jnp.log(jnp.maximum(fg, HG_MIN_F))
        kk = one_m_lb * _sigmoid(-z)
        g_hi = logf.astype(BF16)
        g_lo = (logf - g_hi.astype(F32)).astype(BF16)
        bc = _dot(tri, g_hi) + _dot(tri, g_lo)
        tot = bc[chunk - 1:chunk, :]
        mid = 0.5 * tot
        qq = proj_ref[rows, 0:HG_K]
        qs = _silu(qq)
        qt = (qs * jnp.exp(bc - mid)).astype(BF16)
        kt = (kk * jnp.exp(mid - bc)).astype(BF16)
        q2 = (qs * jnp.exp(bc)).astype(BF16)
        k2 = (kk * jnp.exp(tot - bc)).astype(BF16)
        dec = jnp.exp(tot)
        for h in range(HG_HEADS):
            sl = slice(h * HG_DK, (h + 1) * HG_DK)
            vb = proj_ref[rows, 2 * HG_K + h * HG_DV:2 * HG_K + (h + 1) * HG_DV].astype(BF16)
            gate = proj_ref[rows, 2 * HG_K + HG_V + h * HG_DV:2 * HG_K + HG_V + (h + 1) * HG_DV]
            attn = jnp.where(causal, _dot_nt(qt[:, sl], kt[:, sl]), 0.0)
            st = st_ref[h]
            o = _dot(attn.astype(BF16), vb) + _dot_nt(q2[:, sl], st.astype(BF16))
            st_ref[h] = st * dec[:, sl] + _dot_tn(vb, k2[:, sl])
            o = o * lax.rsqrt(jnp.mean(o * o, axis=-1, keepdims=True) + EPS)
            o_ref[0, rows, h * HG_DV:(h + 1) * HG_DV] = (
                o * nw[:, h * HG_DV:(h + 1) * HG_DV] * _silu(gate)).astype(o_ref.dtype)
        return carry

    lax.fori_loop(0, n_chunks, body, 0)

    @pl.when(t == pl.num_programs(1) - 1)
    def _():
        for h in range(HG_HEADS):
            s_ref[0, h] = st_ref[h].T


def _hgrn(hn, w, hg_lb, norm_w, s0, tb, chunk, layer):
    b, t, _ = hn.shape
    nt = t // tb
    depth = hg_lb.shape[0]
    return pl.pallas_call(
        functools.partial(_hg_kernel, chunk=chunk, n_chunks=tb // chunk, layer=layer, depth=depth),
        grid=(b, nt),
        in_specs=[
            pl.BlockSpec((1, tb, D_MODEL), lambda i, j: (i, j, 0)),
            _resident((D_MODEL, HG_COLS)),
            _resident((depth, HG_K)),
            _resident((1, HG_V)),
            pl.BlockSpec((1, HG_HEADS, HG_DK, HG_DV), lambda i, j: (i, 0, 0, 0)),
        ],
        out_specs=[
            pl.BlockSpec((1, tb, HG_V), lambda i, j: (i, j, 0)),
            pl.BlockSpec((1, HG_HEADS, HG_DK, HG_DV), lambda i, j: (i, 0, 0, 0)),
        ],
        out_shape=[
            jax.ShapeDtypeStruct((b, t, HG_V), BF16),
            jax.ShapeDtypeStruct((b, HG_HEADS, HG_DK, HG_DV), F32),
        ],
        scratch_shapes=[pltpu.VMEM((tb, HG_COLS), F32), pltpu.VMEM((HG_HEADS, HG_DV, HG_DK), F32)],
        compiler_params=_cparams(("parallel", "arbitrary")),
        name="hgrn2",
    )(hn, w, hg_lb, norm_w.reshape(1, HG_V), s0)


def _rg_kernel(hn_ref, w_ref, cw_ref, cb_ref, wr_ref, br_ref, wi_ref, bi_ref, lam_ref, h0_ref, buf0_ref,
               o_ref, h_out_ref, buf_out_ref, proj_ref, ubuf_ref, a_ref, b_ref, h_ref, *, tb, first_pos_zero):
    t = pl.program_id(1)
    hist = SUBLANES

    @pl.when(t == 0)
    def _():
        ubuf_ref[0:hist, :] = jnp.zeros((hist, RG_WIDTH), F32)
        ubuf_ref[hist - (RG_CONV - 1):hist, :] = buf0_ref[0]
        h_ref[...] = h0_ref[0]

    proj_ref[...] = _dot(hn_ref[0], w_ref[...])
    ubuf_ref[hist:hist + tb, :] = proj_ref[:, 0:RG_WIDTH]

    lam = lam_ref[...]
    sp = jnp.maximum(-lam, 0.0) + jnp.log1p(jnp.exp(-jnp.abs(lam)))
    first_row = lax.broadcasted_iota(jnp.int32, (tb, RG_BLOCK), 0) == 0

    for nb in range(RG_BLOCKS):
        sl = slice(nb * RG_BLOCK, (nb + 1) * RG_BLOCK)
        xc = cb_ref[:, sl]
        for j in range(RG_CONV):
            xc = xc + cw_ref[j:j + 1, sl] * ubuf_ref[hist - (RG_CONV - 1) + j:hist - (RG_CONV - 1) + j + tb, sl]
        xcb = xc.astype(BF16)
        r = _sigmoid(_dot(xcb, wr_ref[nb]) + br_ref[:, sl])
        ig = _sigmoid(_dot(xcb, wi_ref[nb]) + bi_ref[:, sl])
        log_a = -RG_C * r * sp[:, sl]
        th = jnp.tanh(log_a)
        mult = jnp.sqrt(jnp.maximum(-2.0 * th / (1.0 - th), 0.0))
        if first_pos_zero:
            mult = jnp.where(jnp.logical_and(first_row, t == 0), 1.0, mult)
        a_ref[:, sl] = jnp.exp(log_a)
        b_ref[:, sl] = mult * (ig * xc)

    def step(i, h):
        h = a_ref[pl.ds(i, 1), :] * h + b_ref[pl.ds(i, 1), :]
        b_ref[pl.ds(i, 1), :] = h
        return h

    h_last = lax.fori_loop(0, tb, step, h_ref[...], unroll=8)
    h_ref[...] = h_last

    o_ref[0] = (b_ref[...] * _gelu(proj_ref[:, RG_WIDTH:2 * RG_WIDTH])).astype(o_ref.dtype)

    @pl.when(t == pl.num_programs(1) - 1)
    def _():
        h_out_ref[0] = h_last
        buf_out_ref[0] = ubuf_ref[tb + hist - (RG_CONV - 1):tb + hist, :]

    ubuf_ref[0:hist, :] = ubuf_ref[tb:tb + hist, :]


def _rglru(hn, w, cw, cb, wr, br, wi, bi, lam, h0, buf0, tb, pos0):
    b, t, _ = hn.shape
    nt = t // tb
    vec = lambda a: a.reshape(1, RG_WIDTH)
    return pl.pallas_call(
        functools.partial(_rg_kernel, tb=tb, first_pos_zero=(pos0 == 0)),
        grid=(b, nt),
        in_specs=[
            pl.BlockSpec((1, tb, D_MODEL), lambda i, j: (i, j, 0)),
            _resident((D_MODEL, RG_COLS)),
            _resident((RG_CONV, RG_WIDTH)),
            _resident((1, RG_WIDTH)),
            _resident((RG_BLOCKS, RG_BLOCK, RG_BLOCK)),
            _resident((1, RG_WIDTH)),
            _resident((RG_BLOCKS, RG_BLOCK, RG_BLOCK)),
            _resident((1, RG_WIDTH)),
            _resident((1, RG_WIDTH)),
            pl.BlockSpec((1, 1, RG_WIDTH), lambda i, j: (i, 0, 0)),
            pl.BlockSpec((1, RG_CONV - 1, RG_WIDTH), lambda i, j: (i, 0, 0)),
        ],
        out_specs=[
            pl.BlockSpec((1, tb, RG_WIDTH), lambda i, j: (i, j, 0)),
            pl.BlockSpec((1, 1, RG_WIDTH), lambda i, j: (i, 0, 0)),
            pl.BlockSpec((1, RG_CONV - 1, RG_WIDTH), lambda i, j: (i, 0, 0)),
        ],
        out_shape=[
            jax.ShapeDtypeStruct((b, t, RG_WIDTH), BF16),
            jax.ShapeDtypeStruct((b, 1, RG_WIDTH), F32),
            jax.ShapeDtypeStruct((b, RG_CONV - 1, RG_WIDTH), F32),
        ],
        scratch_shapes=[
            pltpu.VMEM((tb, RG_COLS), F32),
            pltpu.VMEM((tb + SUBLANES, RG_WIDTH), F32),
            pltpu.VMEM((tb, RG_WIDTH), F32),
            pltpu.VMEM((tb, RG_WIDTH), F32),
            pltpu.VMEM((1, RG_WIDTH), F32),
        ],
        compiler_params=_cparams(("parallel", "arbitrary")),
        name="rglru",
    )(hn, w, cw, vec(cb), wr, vec(br), wi, vec(bi), vec(lam), h0.reshape(b, 1, RG_WIDTH), buf0)


def _merge_kernel(hn_ref, oret_ref, ohg_ref, org_ref, x_ref, wg_ref, wb_ref, wo_ref, out_ref):
    hn = hn_ref[...]
    mixed = _sigmoid(_dot(hn, wg_ref[:, 0:D_MODEL])) * _dot(oret_ref[...], wb_ref[0:RET_V, :])
    mixed = mixed + _sigmoid(_dot(hn, wg_ref[:, D_MODEL:2 * D_MODEL])) * _dot(
        ohg_ref[...], wb_ref[RET_V:RET_V + HG_V, :])
    mixed = mixed + _sigmoid(_dot(hn, wg_ref[:, 2 * D_MODEL:3 * D_MODEL])) * _dot(
        org_ref[...], wb_ref[RET_V + HG_V:RET_V + HG_V + RG_WIDTH, :])
    out_ref[...] = x_ref[...] + _dot(mixed.astype(BF16), wo_ref[...])


def _merge(hn, o_ret, o_hg, o_rg, x, wg, wb, wo, tm):
    rows = x.shape[0]
    rowspec = lambda width: pl.BlockSpec((tm, width), lambda i: (i, 0))
    return pl.pallas_call(
        _merge_kernel,
        grid=(rows // tm,),
        in_specs=[rowspec(D_MODEL), rowspec(RET_V), rowspec(HG_V), rowspec(RG_WIDTH), rowspec(D_MODEL),
                  _resident((D_MODEL, GATE_COLS)), _resident((RET_V + HG_V + RG_WIDTH, D_MODEL)),
                  _resident((D_MODEL, D_MODEL))],
        out_specs=rowspec(D_MODEL),
        out_shape=jax.ShapeDtypeStruct((rows, D_MODEL), F32),
        compiler_params=_cparams(("parallel",)),
        name="merge",
    )(hn, o_ret, o_hg, o_rg, x, wg, wb, wo)


def _ffn_kernel(x_ref, n2_ref, wup_ref, cw_ref, cb_ref, wdn_ref, buf0_ref, fn_ref, out_ref, buf_out_ref,
                abuf_ref, *, tb, final_norm):
    t = pl.program_id(1)
    hist = SUBLANES

    @pl.when(t == 0)
    def _():
        abuf_ref[0:hist, :] = jnp.zeros((hist, D_FF), F32)
        abuf_ref[hist - (FFN_CONV - 1):hist, :] = buf0_ref[0]

    x = x_ref[0]
    hn = _rms(x, n2_ref[...]).astype(BF16)
    abuf_ref[hist:hist + tb, :] = _dot(hn, wup_ref[:, 0:D_FF])
    gate = _dot(hn, wup_ref[:, D_FF:2 * D_FF])
    a = cb_ref[...]
    for j in range(FFN_CONV):
        a = a + cw_ref[j:j + 1, :] * abuf_ref[hist - (FFN_CONV - 1) + j:hist - (FFN_CONV - 1) + j + tb, :]
    y = x + _dot((_gelu(a) * gate).astype(BF16), wdn_ref[...])
    if final_norm:
        y = _rms(y, fn_ref[...])
    out_ref[0] = y

    @pl.when(t == pl.num_programs(1) - 1)
    def _():
        buf_out_ref[0] = abuf_ref[tb + hist - (FFN_CONV - 1):tb + hist, :]

    abuf_ref[0:hist, :] = abuf_ref[tb:tb + hist, :]


def _convffn(x, n2, wup, cw, cb, wdn, buf0, fn, tb, final_norm):
    b, t, _ = x.shape
    nt = t // tb
    return pl.pallas_call(
        functools.partial(_ffn_kernel, tb=tb, final_norm=final_norm),
        grid=(b, nt),
        in_specs=[
            pl.BlockSpec((1, tb, D_MODEL), lambda i, j: (i, j, 0)),
            _resident((1, D_MODEL)),
            _resident((D_MODEL, 2 * D_FF)),
            _resident((FFN_CONV, D_FF)),
            _resident((1, D_FF)),
            _resident((D_FF, D_MODEL)),
            pl.BlockSpec((1, FFN_CONV - 1, D_FF), lambda i, j: (i, 0, 0)),
            _resident((1, D_MODEL)),
        ],
        out_specs=[
            pl.BlockSpec((1, tb, D_MODEL), lambda i, j: (i, j, 0)),
            pl.BlockSpec((1, FFN_CONV - 1, D_FF), lambda i, j: (i, 0, 0)),
        ],
        out_shape=[
            jax.ShapeDtypeStruct((b, t, D_MODEL), F32),
            jax.ShapeDtypeStruct((b, FFN_CONV - 1, D_FF), F32),
        ],
        scratch_shapes=[pltpu.VMEM((tb + SUBLANES, D_FF), F32)],
        compiler_params=_cparams(("parallel", "arbitrary")),
        name="convffn",
    )(x, n2.reshape(1, D_MODEL), wup, cw, cb.reshape(1, D_FF), wdn, buf0, fn.reshape(1, D_MODEL))


def _rope_tables(t, pos0):
    half = RET_DK // 2
    inv = jnp.power(ROPE_BASE, -jnp.arange(half, dtype=F32) / half)
    ang = (jnp.arange(t, dtype=F32) + pos0)[:, None] * inv[None, :]
    cos, sin = jnp.cos(ang), jnp.sin(ang)
    return jnp.concatenate([cos, cos], axis=-1), jnp.concatenate([-sin, sin], axis=-1)


def _block_sizes(t):
    if t <= CHUNK:
        return t, t
    return 256, CHUNK


def _trunk(x, pos0, r0, s0, h0, rgb0, ffb0, p):
    b, t, _ = x.shape
    tb, chunk = _block_sizes(t)
    rows = b * t
    tm = min(rows, 512)
    cos2, sin2 = _rope_tables(t, pos0)
    depth = p["norm1_w"].shape[0]
    states = []
    x2d = x.reshape(rows, D_MODEL)
    for l in range(depth):
        hn2d = _norm(x2d, p["norm1_w"][l], tm)
        hn = hn2d.reshape(b, t, D_MODEL)
        o_ret, r_new = _retention(hn, p["w_ret"][l], cos2, sin2, r0[l], tb, chunk)
        o_hg, s_new = _hgrn(hn, p["w_hg"][l], p["hg_lb"], p["hg_norm_w"][l], s0[l], tb, chunk, l)
        o_rg, h_new, rgb_new = _rglru(
            hn, p["w_rg"][l], p["rg_conv_w"][l], p["rg_conv_b"][l], p["rg_w_r"][l], p["rg_b_r"][l],
            p["rg_w_i"][l], p["rg_b_i"][l], p["rg_lambda"][l], h0[l], rgb0[l], tb, pos0)
        x2d = _merge(hn2d, o_ret.reshape(rows, RET_V), o_hg.reshape(rows, HG_V), o_rg.reshape(rows, RG_WIDTH),
                     x2d, p["w_gate"][l], p["w_branch"][l], p["w_out"][l], tm)
        x3d, ffb_new = _convffn(
            x2d.reshape(b, t, D_MODEL), p["norm2_w"][l], p["w_up"][l], p["ffn_conv_w"][l], p["ffn_conv_b"][l],
            p["w_down"][l], ffb0[l], p["final_norm_w"], tb, l == depth - 1)
        x2d = x3d.reshape(rows, D_MODEL)
        states.append((r_new, s_new, h_new.reshape(b, RG_WIDTH), rgb_new, ffb_new))
    new_states = tuple(jnp.stack(st, axis=0) for st in zip(*states))
    return x2d.reshape(b, t, D_MODEL), new_states


def kernel(x_prompt, x_sample, state_ret, state_hgrn, state_rglru, cache_rg_conv, cache_ffn_conv,
           norm1_w, w_in, w_branch, w_out, rg_conv_w, rg_conv_b, rg_w_r, rg_b_r, rg_w_i, rg_b_i,
           rg_lambda, hg_lb, hg_norm_w, norm2_w, w_up, ffn_conv_w, ffn_conv_b, w_down, final_norm_w):
    depth = w_in.shape[0]
    batch = x_prompt.shape[0]
    bf = lambda a: a.astype(BF16)
    p = dict(
        norm1_w=norm1_w, norm2_w=norm2_w, final_norm_w=final_norm_w, hg_lb=hg_lb, hg_norm_w=hg_norm_w,
        w_ret=bf(w_in[:, :, 0:OFF_HG]), w_hg=bf(w_in[:, :, OFF_HG:OFF_RG]),
        w_rg=bf(w_in[:, :, OFF_RG:OFF_GATE]), w_gate=bf(w_in[:, :, OFF_GATE:OFF_GATE + GATE_COLS]),
        w_branch=bf(w_branch), w_out=bf(w_out), rg_conv_w=rg_conv_w, rg_conv_b=rg_conv_b,
        rg_w_r=bf(rg_w_r), rg_b_r=rg_b_r, rg_w_i=bf(rg_w_i), rg_b_i=rg_b_i, rg_lambda=rg_lambda,
        w_up=bf(w_up), ffn_conv_w=ffn_conv_w, ffn_conv_b=ffn_conv_b, w_down=bf(w_down),
    )
    zeros = lambda *shape: jnp.zeros((depth, batch) + shape, F32)
    y_p, (ret_p, hg_p, rgh_p, rgc_p, ffc_p) = _trunk(
        x_prompt, 0, zeros(RET_HEADS, RET_DK, RET_DV), zeros(HG_HEADS, HG_DK, HG_DV), zeros(RG_WIDTH),
        zeros(RG_CONV - 1, RG_WIDTH), zeros(FFN_CONV - 1, D_FF), p)
    y_s, (ret_s, hg_s, rgh_s, rgc_s, ffc_s) = _trunk(
        x_sample, PAST_LEN, state_ret, state_hgrn, state_rglru, cache_rg_conv, cache_ffn_conv, p)
    return (y_p, y_s, ret_p, ret_s, hg_p, hg_s, rgh_p, rgh_s, rgc_p, rgc_s, ffc_p, ffc_s)
```

```python
import functools
import math

import jax
import jax.numpy as jnp
from jax import lax
from jax.experimental import pallas as pl
from jax.experimental.pallas import tpu as pltpu

F32 = jnp.float32
BF16 = jnp.bfloat16

D_MODEL = 1024
PAST_LEN = 2048
CHUNK = 64
RET_HEADS, RET_DK, RET_DV = 4, 128, 256
RET_QK = RET_HEADS * RET_DK
RET_V = RET_HEADS * RET_DV
ROPE_BASE = 10000.0
HG_HEADS, HG_DK, HG_DV = 8, 128, 128
HG_K = HG_HEADS * HG_DK
HG_V = HG_HEADS * HG_DV
HG_MIN_F = 1e-6
RG_BLOCKS, RG_BLOCK = 5, 256
RG_WIDTH = RG_BLOCKS * RG_BLOCK
RG_CONV = 4
RG_C = 8.0
D_FF = 2816
FFN_CONV = 3
EPS = 1e-6

RET_COLS = 2 * RET_QK + 2 * RET_V
HG_COLS = 2 * HG_K + 2 * HG_V
RG_COLS = 2 * RG_WIDTH
GATE_COLS = 3 * D_MODEL
OFF_HG = RET_COLS
OFF_RG = OFF_HG + HG_COLS
OFF_GATE = OFF_RG + RG_COLS

VMEM_LIMIT_BYTES = 56 * 1024 * 1024
SUBLANES = 8
LANES = 128
TIME_BLOCK = 256
ROW_BLOCK = 512
GELU_C = math.sqrt(2.0 / math.pi)

LOG_GAMMA = tuple(math.log1p(-(2.0 ** (-5.0 - h))) for h in range(RET_HEADS))


def _cparams(sem):
    return pltpu.CompilerParams(dimension_semantics=sem, vmem_limit_bytes=VMEM_LIMIT_BYTES)


def _resident(shape):
    nd = len(shape)
    return pl.BlockSpec(shape, lambda *_: (0,) * nd, pipeline_mode=pl.Buffered(1))


def _sigmoid_pair(z):
    e = jnp.exp(-jnp.abs(z))
    d = pl.reciprocal(1.0 + e, approx=True)
    ed = e * d
    pos = z >= 0.0
    return jnp.where(pos, d, ed), jnp.where(pos, ed, d)


def _half_tanh_half(x):
    return jnp.tanh(0.5 * x)


def _sigmoid(x):
    return 0.5 * (1.0 + _half_tanh_half(x))


def _silu(x):
    return (0.5 * x) * (1.0 + _half_tanh_half(x))


def _gelu(x):
    return 0.5 * x * (1.0 + jnp.tanh(GELU_C * (x + 0.044715 * (x * x * x))))


def _rms(x, w):
    ms = jnp.mean(x * x, axis=-1, keepdims=True)
    return x * lax.rsqrt(ms + EPS) * w


def _dot(a, b):
    return jnp.dot(a, b, preferred_element_type=F32)


def _dot_nt(a, b):
    return lax.dot_general(a, b, (((1,), (1,)), ((), ())), preferred_element_type=F32)


def _dot_tn(a, b):
    return lax.dot_general(a, b, (((0,), (0,)), ((), ())), preferred_element_type=F32)


def _norm_kernel(x_ref, w_ref, o_ref):
    o_ref[...] = _rms(x_ref[...], w_ref[...]).astype(o_ref.dtype)


def _norm(x2d, w, tm):
    rows = x2d.shape[0]
    return pl.pallas_call(
        _norm_kernel,
        grid=(rows // tm,),
        in_specs=[pl.BlockSpec((tm, D_MODEL), lambda i: (i, 0)), _resident((1, D_MODEL))],
        out_specs=pl.BlockSpec((tm, D_MODEL), lambda i: (i, 0)),
        out_shape=jax.ShapeDtypeStruct((rows, D_MODEL), BF16),
        compiler_params=_cparams(("parallel",)),
        name="norm",
    )(x2d, w.reshape(1, D_MODEL))


def _ret_kernel(hn_ref, w_ref, cos_ref, sin_ref, r0_ref, o_ref, r_ref, proj_ref, dec_ref, qd_ref, kd_ref,
                *, tb, chunk):
    t = pl.program_id(1)

    @pl.when(t == 0)
    def _():
        r_ref[...] = r0_ref[...]
        row = lax.broadcasted_iota(jnp.int32, (tb, tb), 0)
        col = lax.broadcasted_iota(jnp.int32, (tb, tb), 1)
        dist = jnp.abs(row - col).astype(F32)
        chunk_bits = chunk.bit_length() - 1
        later_chunk = col >= ((row >> chunk_bits) + 1) * chunk
        n = lax.broadcasted_iota(jnp.int32, (tb, RET_DK), 0).astype(F32)
        for h in range(RET_HEADS):
            lg = LOG_GAMMA[h]
            dec_ref[h] = jnp.where(later_chunk, 0.0, jnp.exp(dist * lg))
            qd_ref[h] = jnp.exp(lg * (n + 1.0))
            kd_ref[h] = jnp.exp(lg * (float(tb) - 1.0 - n))

    proj_ref[...] = _dot(hn_ref[0], w_ref[...])
    cos = cos_ref[...]
    sin = sin_ref[...]
    for h in range(RET_HEADS):
        q = proj_ref[:, h * RET_DK:(h + 1) * RET_DK]
        k = proj_ref[:, RET_QK + h * RET_DK:RET_QK + (h + 1) * RET_DK]
        vb = proj_ref[:, 2 * RET_QK + h * RET_DV:2 * RET_QK + (h + 1) * RET_DV].astype(BF16)
        q = q * cos + pltpu.roll(q, RET_DK // 2, 1) * sin
        k = (k * cos + pltpu.roll(k, RET_DK // 2, 1) * sin) * (RET_DK ** -0.5)
        scores = _dot_nt(q.astype(BF16), k.astype(BF16)) * dec_ref[h]
        r = r_ref[0, h]
        o = _dot(scores.astype(BF16), vb) + _dot((q * qd_ref[h]).astype(BF16), r.astype(BF16))
        r_ref[0, h] = math.exp(LOG_GAMMA[h] * tb) * r + _dot_tn((k * kd_ref[h]).astype(BF16), vb)
        o = o * lax.rsqrt(jnp.mean(o * o, axis=-1, keepdims=True) + EPS)
        g = proj_ref[:, 2 * RET_QK + RET_V + h * RET_DV:2 * RET_QK + RET_V + (h + 1) * RET_DV]
        o_ref[0, :, h * RET_DV:(h + 1) * RET_DV] = (o * _silu(g)).astype(o_ref.dtype)


def _retention(hn, w, cos2, sin2, r0, tb, chunk):
    b, t, _ = hn.shape
    nt = t // tb
    return pl.pallas_call(
        functools.partial(_ret_kernel, tb=tb, chunk=chunk),
        grid=(b, nt),
        in_specs=[
            pl.BlockSpec((1, tb, D_MODEL), lambda i, j: (i, j, 0)),
            _resident((D_MODEL, RET_COLS)),
            pl.BlockSpec((tb, RET_DK), lambda i, j: (j, 0)),
            pl.BlockSpec((tb, RET_DK), lambda i, j: (j, 0)),
            pl.BlockSpec((1, RET_HEADS, RET_DK, RET_DV), lambda i, j: (i, 0, 0, 0)),
        ],
        out_specs=[
            pl.BlockSpec((1, tb, RET_V), lambda i, j: (i, j, 0)),
            pl.BlockSpec((1, RET_HEADS, RET_DK, RET_DV), lambda i, j: (i, 0, 0, 0)),
        ],
        out_shape=[
            jax.ShapeDtypeStruct((b, t, RET_V), BF16),
            jax.ShapeDtypeStruct((b, RET_HEADS, RET_DK, RET_DV), F32),
        ],
        scratch_shapes=[
            pltpu.VMEM((tb, RET_COLS), F32),
            pltpu.VMEM((RET_HEADS, tb, tb), F32),
            pltpu.VMEM((RET_HEADS, tb, RET_DK), F32),
            pltpu.VMEM((RET_HEADS, tb, RET_DK), F32),
        ],
        compiler_params=_cparams(("parallel", "arbitrary")),
        name="retention",
    )(hn, w, cos2, sin2, r0)


def _hg_kernel(hn_ref, w_ref, lb_ref, nw_ref, s0_ref, o_ref, s_ref, proj_ref, st_ref,
               *, chunk, n_chunks, layer, depth):
    t = pl.program_id(1)

    @pl.when(t == 0)
    def _():
        for h in range(HG_HEADS):
            st_ref[h] = s0_ref[0, h].T

    proj_ref[...] = _dot(hn_ref[0], w_ref[...])

    lrows = [lb_ref[d:d + 1, :] for d in range(depth)]
    mx = functools.reduce(jnp.maximum, lrows)
    ex = [jnp.exp(r - mx) for r in lrows]
    den = functools.reduce(lambda a, c: a + c, ex)
    lb = jnp.zeros_like(mx)
    for d in range(1, layer + 1):
        lb = lb + ex[d] / den
    one_m_lb = 1.0 - lb

    row = lax.broadcasted_iota(jnp.int32, (chunk, chunk), 0)
    col = lax.broadcasted_iota(jnp.int32, (chunk, chunk), 1)
    causal = row >= col
    tri = jnp.where(causal, 1.0, 0.0).astype(BF16)
    nw = nw_ref[...]

    for c in range(n_chunks):
        rows = slice(c * chunk, (c + 1) * chunk)
        sg, sgn = _sigmoid_pair(proj_ref[rows, HG_K:2 * HG_K])
        if layer == 0:
            fg, kk = sg, sgn
        else:
            fg, kk = lb + one_m_lb * sg, one_m_lb * sgn
        logf = jnp.log(jnp.maximum(fg, HG_MIN_F))
        g_hi = logf.astype(BF16)
        g_lo = (logf - g_hi.astype(F32)).astype(BF16)
        bc = _dot(tri, g_hi) + _dot(tri, g_lo)
        mid = 0.5 * bc[chunk - 1:chunk, :]
        emid = jnp.exp(mid)
        bm = bc - mid
        qt = (_silu(proj_ref[rows, 0:HG_K]) * jnp.exp(bm)).astype(BF16)
        kt = (kk * jnp.exp(-bm)).astype(BF16)
        for h in range(HG_HEADS):
            sl = slice(h * HG_DK, (h + 1) * HG_DK)
            vb = proj_ref[rows, 2 * HG_K + h * HG_DV:2 * HG_K + (h + 1) * HG_DV].astype(BF16)
            gate = proj_ref[rows, 2 * HG_K + HG_V + h * HG_DV:2 * HG_K + HG_V + (h + 1) * HG_DV]
            attn = jnp.where(causal, _dot_nt(qt[:, sl], kt[:, sl]), 0.0)
            st_e = st_ref[h] * emid[:, sl]
            o = _dot(attn.astype(BF16), vb) + _dot_nt(qt[:, sl], st_e.astype(BF16))
            st_ref[h] = (st_e + _dot_tn(vb, kt[:, sl])) * emid[:, sl]
            o = o * lax.rsqrt(jnp.mean(o * o, axis=-1, keepdims=True) + EPS)
            o_ref[0, rows, h * HG_DV:(h + 1) * HG_DV] = (
                o * nw[:, h * HG_DV:(h + 1) * HG_DV] * _silu(gate)).astype(o_ref.dtype)

    @pl.when(t == pl.num_programs(1) - 1)
    def _():
        for h in range(HG_HEADS):
            s_ref[0, h] = st_ref[h].T


def _hgrn(hn, w, hg_lb, norm_w, s0, tb, chunk, layer):
    b, t, _ = hn.shape
    nt = t // tb
    depth = hg_lb.shape[0]
    return pl.pallas_call(
        functools.partial(_hg_kernel, chunk=chunk, n_chunks=tb // chunk, layer=layer, depth=depth),
        grid=(b, nt),
        in_specs=[
            pl.BlockSpec((1, tb, D_MODEL), lambda i, j: (i, j, 0)),
            _resident((D_MODEL, HG_COLS)),
            _resident((depth, HG_K)),
            _resident((1, HG_V)),
            pl.BlockSpec((1, HG_HEADS, HG_DK, HG_DV), lambda i, j: (i, 0, 0, 0)),
        ],
        out_specs=[
            pl.BlockSpec((1, tb, HG_V), lambda i, j: (i, j, 0)),
            pl.BlockSpec((1, HG_HEADS, HG_DK, HG_DV), lambda i, j: (i, 0, 0, 0)),
        ],
        out_shape=[
            jax.ShapeDtypeStruct((b, t, HG_V), BF16),
            jax.ShapeDtypeStruct((b, HG_HEADS, HG_DK, HG_DV), F32),
        ],
        scratch_shapes=[pltpu.VMEM((tb, HG_COLS), F32), pltpu.VMEM((HG_HEADS, HG_DV, HG_DK), F32)],
        compiler_params=_cparams(("parallel", "arbitrary")),
        name="hgrn2",
    )(hn, w, hg_lb, norm_w.reshape(1, HG_V), s0)


def _rg_kernel(hn_ref, w_ref, cw_ref, cb_ref, wr_ref, br_ref, wi_ref, bi_ref, lam_ref, h0_ref, buf0_ref,
               o_ref, h_out_ref, buf_out_ref, proj_ref, ubuf_ref, a_ref, b_ref, perm_ref, tail_ref, cin_ref, h_ref,
               *, tb, first_pos_zero, interleaved):
    t = pl.program_id(1)
    seg = tb // SUBLANES
    step = SUBLANES if interleaved else 1
    hist = -(-(RG_CONV - 1) * step // SUBLANES) * SUBLANES

    @pl.when(t == 0)
    def _():
        tail_ref[...] = buf0_ref[0]
        h_ref[...] = h0_ref[0]
        if interleaved:
            row = lax.broadcasted_iota(jnp.int32, (tb, tb), 0)
            col = lax.broadcasted_iota(jnp.int32, (tb, tb), 1)
            frame_of = lambda p: jnp.bitwise_and(p, SUBLANES - 1) * seg + (p >> (SUBLANES.bit_length() - 1))
            perm_ref[0] = jnp.where(col == frame_of(row), 1.0, 0.0).astype(BF16)
            perm_ref[1] = jnp.where(row == frame_of(col), 1.0, 0.0).astype(BF16)

    hn = hn_ref[0]
    if interleaved:
        hn = _dot(perm_ref[0], hn).astype(BF16)
    proj_ref[...] = _dot(hn, w_ref[...])
    ubuf_ref[hist:hist + tb, :] = proj_ref[:, 0:RG_WIDTH]
    n_hist = RG_CONV - 1
    if interleaved:
        row0 = lax.broadcasted_iota(jnp.int32, (SUBLANES, RG_WIDTH), 0) == 0
        for k in range(n_hist):
            src = hist + (seg - n_hist + k) * SUBLANES
            moved = pltpu.roll(ubuf_ref[src:src + SUBLANES, :], 1, 0)
            ubuf_ref[k * SUBLANES:(k + 1) * SUBLANES, :] = jnp.where(row0, tail_ref[k:k + 1, :], moved)
        for k in range(n_hist):
            last = hist + (seg - n_hist + k) * SUBLANES + SUBLANES - 1
            tail_ref[k:k + 1, :] = ubuf_ref[last:last + 1, :]
    else:
        ubuf_ref[hist - n_hist:hist, :] = tail_ref[...]
        tail_ref[...] = ubuf_ref[hist + tb - n_hist:hist + tb, :]

    lam = lam_ref[...]
    neg_half_c_sp = (-0.5 * RG_C) * (jnp.maximum(-lam, 0.0) + jnp.log1p(jnp.exp(-jnp.abs(lam))))
    first_row = lax.broadcasted_iota(jnp.int32, (tb, RG_BLOCK), 0) == 0

    for nb in range(RG_BLOCKS):
        sl = slice(nb * RG_BLOCK, (nb + 1) * RG_BLOCK)
        xc = cb_ref[:, sl]
        for j in range(RG_CONV):
            lo = hist - (n_hist - j) * step
            xc = xc + cw_ref[j:j + 1, sl] * ubuf_ref[lo:lo + tb, sl]
        xcb = xc.astype(BF16)
        tr = _half_tanh_half(_dot(xcb, wr_ref[nb]) + br_ref[:, sl])
        ti = _half_tanh_half(_dot(xcb, wi_ref[nb]) + bi_ref[:, sl])
        log_a = neg_half_c_sp[:, sl] * (1.0 + tr)
        th = jnp.tanh(log_a)
        mult = jnp.sqrt(jnp.maximum(-2.0 * th * pl.reciprocal(1.0 - th, approx=True), 0.0))
        if first_pos_zero:
            mult = jnp.where(jnp.logical_and(first_row, t == 0), 1.0, mult)
        a_ref[:, sl] = jnp.exp(log_a)
        b_ref[:, sl] = mult * ((0.5 * xc) * (1.0 + ti))

    if interleaved:
        def tile_scan(i, c):
            hloc, prod = c
            rows = pl.ds(pl.multiple_of(i * SUBLANES, SUBLANES), SUBLANES)
            a = a_ref[rows, :]
            hloc = a * hloc + b_ref[rows, :]
            prod = a * prod
            b_ref[rows, :] = hloc
            a_ref[rows, :] = prod
            return hloc, prod

        init = (jnp.zeros((SUBLANES, RG_WIDTH), F32), jnp.ones((SUBLANES, RG_WIDTH), F32))
        hloc, prod = lax.fori_loop(0, seg, tile_scan, init, unroll=4)
        carry = h_ref[...]
        for s in range(SUBLANES):
            cin_ref[s:s + 1, :] = carry
            carry = hloc[s:s + 1, :] + prod[s:s + 1, :] * carry
        h_ref[...] = carry
        cin = cin_ref[...]

        def tile_out(i, _):
            rows = pl.ds(pl.multiple_of(i * SUBLANES, SUBLANES), SUBLANES)
            h = b_ref[rows, :] + a_ref[rows, :] * cin
            b_ref[rows, :] = h * _gelu(proj_ref[rows, RG_WIDTH:2 * RG_WIDTH])
            return 0

        lax.fori_loop(0, seg, tile_out, 0, unroll=4)
        o_ref[0] = _dot(perm_ref[1], b_ref[...].astype(BF16)).astype(o_ref.dtype)
    else:
        def step_scan(i, h):
            h = a_ref[pl.ds(i, 1), :] * h + b_ref[pl.ds(i, 1), :]
            b_ref[pl.ds(i, 1), :] = h
            return h

        h_ref[...] = lax.fori_loop(0, tb, step_scan, h_ref[...], unroll=8)
        o_ref[0] = (b_ref[...] * _gelu(proj_ref[:, RG_WIDTH:2 * RG_WIDTH])).astype(o_ref.dtype)

    @pl.when(t == pl.num_programs(1) - 1)
    def _():
        h_out_ref[0] = h_ref[...]
        buf_out_ref[0] = tail_ref[...]


def _rglru(hn, w, cw, cb, wr, br, wi, bi, lam, h0, buf0, tb, pos0):
    b, t, _ = hn.shape
    nt = t // tb
    interleaved = tb % LANES == 0
    vec = lambda a: a.reshape(1, RG_WIDTH)
    return pl.pallas_call(
        functools.partial(_rg_kernel, tb=tb, first_pos_zero=(pos0 == 0), interleaved=interleaved),
        grid=(b, nt),
        in_specs=[
            pl.BlockSpec((1, tb, D_MODEL), lambda i, j: (i, j, 0)),
            _resident((D_MODEL, RG_COLS)),
            _resident((RG_CONV, RG_WIDTH)),
            _resident((1, RG_WIDTH)),
            _resident((RG_BLOCKS, RG_BLOCK, RG_BLOCK)),
            _resident((1, RG_WIDTH)),
            _resident((RG_BLOCKS, RG_BLOCK, RG_BLOCK)),
            _resident((1, RG_WIDTH)),
            _resident((1, RG_WIDTH)),
            pl.BlockSpec((1, 1, RG_WIDTH), lambda i, j: (i, 0, 0)),
            pl.BlockSpec((1, RG_CONV - 1, RG_WIDTH), lambda i, j: (i, 0, 0)),
        ],
        out_specs=[
            pl.BlockSpec((1, tb, RG_WIDTH), lambda i, j: (i, j, 0)),
            pl.BlockSpec((1, 1, RG_WIDTH), lambda i, j: (i, 0, 0)),
            pl.BlockSpec((1, RG_CONV - 1, RG_WIDTH), lambda i, j: (i, 0, 0)),
        ],
        out_shape=[
            jax.ShapeDtypeStruct((b, t, RG_WIDTH), BF16),
            jax.ShapeDtypeStruct((b, 1, RG_WIDTH), F32),
            jax.ShapeDtypeStruct((b, RG_CONV - 1, RG_WIDTH), F32),
        ],
        scratch_shapes=[
            pltpu.VMEM((tb, RG_COLS), F32),
            pltpu.VMEM((tb + (RG_CONV - 1) * SUBLANES, RG_WIDTH), F32),
            pltpu.VMEM((tb, RG_WIDTH), F32),
            pltpu.VMEM((tb, RG_WIDTH), F32),
            pltpu.VMEM((2, tb, tb), BF16),
            pltpu.VMEM((RG_CONV - 1, RG_WIDTH), F32),
            pltpu.VMEM((SUBLANES, RG_WIDTH), F32),
            pltpu.VMEM((1, RG_WIDTH), F32),
        ],
        compiler_params=_cparams(("parallel", "arbitrary")),
        name="rglru",
    )(hn, w, cw, vec(cb), wr, vec(br), wi, vec(bi), vec(lam), h0.reshape(b, 1, RG_WIDTH), buf0)


def _merge_kernel(hn_ref, oret_ref, ohg_ref, org_ref, x_ref, wg_ref, wb_ref, wo_ref, out_ref):
    hn = hn_ref[...]
    mixed = _sigmoid(_dot(hn, wg_ref[:, 0:D_MODEL])) * _dot(oret_ref[...], wb_ref[0:RET_V, :])
    mixed = mixed + _sigmoid(_dot(hn, wg_ref[:, D_MODEL:2 * D_MODEL])) * _dot(
        ohg_ref[...], wb_ref[RET_V:RET_V + HG_V, :])
    mixed = mixed + _sigmoid(_dot(hn, wg_ref[:, 2 * D_MODEL:3 * D_MODEL])) * _dot(
        org_ref[...], wb_ref[RET_V + HG_V:RET_V + HG_V + RG_WIDTH, :])
    out_ref[...] = x_ref[...] + _dot(mixed.astype(BF16), wo_ref[...])


def _merge(hn, o_ret, o_hg, o_rg, x, wg, wb, wo, tm):
    rows = x.shape[0]
    rowspec = lambda width: pl.BlockSpec((tm, width), lambda i: (i, 0))
    return pl.pallas_call(
        _merge_kernel,
        grid=(rows // tm,),
        in_specs=[rowspec(D_MODEL), rowspec(RET_V), rowspec(HG_V), rowspec(RG_WIDTH), rowspec(D_MODEL),
                  _resident((D_MODEL, GATE_COLS)), _resident((RET_V + HG_V + RG_WIDTH, D_MODEL)),
                  _resident((D_MODEL, D_MODEL))],
        out_specs=rowspec(D_MODEL),
        out_shape=jax.ShapeDtypeStruct((rows, D_MODEL), F32),
        compiler_params=_cparams(("parallel",)),
        name="merge",
    )(hn, o_ret, o_hg, o_rg, x, wg, wb, wo)


def _ffn_kernel(x_ref, n2_ref, wup_ref, cw_ref, cb_ref, wdn_ref, buf0_ref, fn_ref, out_ref, buf_out_ref,
                abuf_ref, *, tb, final_norm):
    t = pl.program_id(1)
    hist = SUBLANES

    @pl.when(t == 0)
    def _():
        abuf_ref[0:hist, :] = jnp.zeros((hist, D_FF), F32)
        abuf_ref[hist - (FFN_CONV - 1):hist, :] = buf0_ref[0]

    x = x_ref[0]
    hn = _rms(x, n2_ref[...]).astype(BF16)
    abuf_ref[hist:hist + tb, :] = _dot(hn, wup_ref[:, 0:D_FF])
    gate = _dot(hn, wup_ref[:, D_FF:2 * D_FF])
    a = cb_ref[...]
    for j in range(FFN_CONV):
        a = a + cw_ref[j:j + 1, :] * abuf_ref[hist - (FFN_CONV - 1) + j:hist - (FFN_CONV - 1) + j + tb, :]
    y = x + _dot((_gelu(a) * gate).astype(BF16), wdn_ref[...])
    if final_norm:
        y = _rms(y, fn_ref[...])
    out_ref[0] = y

    @pl.when(t == pl.num_programs(1) - 1)
    def _():
        buf_out_ref[0] = abuf_ref[tb + hist - (FFN_CONV - 1):tb + hist, :]

    abuf_ref[0:hist, :] = abuf_ref[tb:tb + hist, :]


def _convffn(x, n2, wup, cw, cb, wdn, buf0, fn, tb, final_norm):
    b, t, _ = x.shape
    nt = t // tb
    return pl.pallas_call(
        functools.partial(_ffn_kernel, tb=tb, final_norm=final_norm),
        grid=(b, nt),
        in_specs=[
            pl.BlockSpec((1, tb, D_MODEL), lambda i, j: (i, j, 0)),
            _resident((1, D_MODEL)),
            _resident((D_MODEL, 2 * D_FF)),
            _resident((FFN_CONV, D_FF)),
            _resident((1, D_FF)),
            _resident((D_FF, D_MODEL)),
            pl.BlockSpec((1, FFN_CONV - 1, D_FF), lambda i, j: (i, 0, 0)),
            _resident((1, D_MODEL)),
        ],
        out_specs=[
            pl.BlockSpec((1, tb, D_MODEL), lambda i, j: (i, j, 0)),
            pl.BlockSpec((1, FFN_CONV - 1, D_FF), lambda i, j: (i, 0, 0)),
        ],
        out_shape=[
            jax.ShapeDtypeStruct((b, t, D_MODEL), F32),
            jax.ShapeDtypeStruct((b, FFN_CONV - 1, D_FF), F32),
        ],
        scratch_shapes=[pltpu.VMEM((tb + SUBLANES, D_FF), F32)],
        compiler_params=_cparams(("parallel", "arbitrary")),
        name="convffn",
    )(x, n2.reshape(1, D_MODEL), wup, cw, cb.reshape(1, D_FF), wdn, buf0, fn.reshape(1, D_MODEL))


def _rope_tables(t, pos0):
    half = RET_DK // 2
    inv = jnp.power(ROPE_BASE, -jnp.arange(half, dtype=F32) / half)
    ang = (jnp.arange(t, dtype=F32) + pos0)[:, None] * inv[None, :]
    cos, sin = jnp.cos(ang), jnp.sin(ang)
    return jnp.concatenate([cos, cos], axis=-1), jnp.concatenate([-sin, sin], axis=-1)


def _block_sizes(t):
    if t <= CHUNK:
        return t, t
    return TIME_BLOCK, CHUNK


def _trunk(x, pos0, r0, s0, h0, rgb0, ffb0, p):
    b, t, _ = x.shape
    tb, chunk = _block_sizes(t)
    rows = b * t
    tm = min(rows, ROW_BLOCK)
    cos2, sin2 = _rope_tables(t, pos0)
    depth = p["norm1_w"].shape[0]
    states = []
    x2d = x.reshape(rows, D_MODEL)
    for l in range(depth):
        hn2d = _norm(x2d, p["norm1_w"][l], tm)
        hn = hn2d.reshape(b, t, D_MODEL)
        o_ret, r_new = _retention(hn, p["w_ret"][l], cos2, sin2, r0[l], tb, chunk)
        o_hg, s_new = _hgrn(hn, p["w_hg"][l], p["hg_lb"], p["hg_norm_w"][l], s0[l], tb, chunk, l)
        o_rg, h_new, rgb_new = _rglru(
            hn, p["w_rg"][l], p["rg_conv_w"][l], p["rg_conv_b"][l], p["rg_w_r"][l], p["rg_b_r"][l],
            p["rg_w_i"][l], p["rg_b_i"][l], p["rg_lambda"][l], h0[l], rgb0[l], tb, pos0)
        x2d = _merge(hn2d, o_ret.reshape(rows, RET_V), o_hg.reshape(rows, HG_V), o_rg.reshape(rows, RG_WIDTH),
                     x2d, p["w_gate"][l], p["w_branch"][l], p["w_out"][l], tm)
        x3d, ffb_new = _convffn(
            x2d.reshape(b, t, D_MODEL), p["norm2_w"][l], p["w_up"][l], p["ffn_conv_w"][l], p["ffn_conv_b"][l],
            p["w_down"][l], ffb0[l], p["final_norm_w"], tb, l == depth - 1)
        x2d = x3d.reshape(rows, D_MODEL)
        states.append((r_new, s_new, h_new.reshape(b, RG_WIDTH), rgb_new, ffb_new))
    new_states = tuple(jnp.stack(st, axis=0) for st in zip(*states))
    return x2d.reshape(b, t, D_MODEL), new_states


def kernel(x_prompt, x_sample, state_ret, state_hgrn, state_rglru, cache_rg_conv, cache_ffn_conv,
           norm1_w, w_in, w_branch, w_out, rg_conv_w, rg_conv_b, rg_w_r, rg_b_r, rg_w_i, rg_b_i,
           rg_lambda, hg_lb, hg_norm_w, norm2_w, w_up, ffn_conv_w, ffn_conv_b, w_down, final_norm_w):
    depth = w_in.shape[0]
    batch = x_prompt.shape[0]
    bf = lambda a: a.astype(BF16)
    p = dict(
        norm1_w=norm1_w, norm2_w=norm2_w, final_norm_w=final_norm_w, hg_lb=hg_lb, hg_norm_w=hg_norm_w,
        w_ret=bf(w_in[:, :, 0:OFF_HG]), w_hg=bf(w_in[:, :, OFF_HG:OFF_RG]),
        w_rg=bf(w_in[:, :, OFF_RG:OFF_GATE]), w_gate=bf(w_in[:, :, OFF_GATE:OFF_GATE + GATE_COLS]),
        w_branch=bf(w_branch), w_out=bf(w_out), rg_conv_w=rg_conv_w, rg_conv_b=rg_conv_b,
        rg_w_r=bf(rg_w_r), rg_b_r=rg_b_r, rg_w_i=bf(rg_w_i), rg_b_i=rg_b_i, rg_lambda=rg_lambda,
        w_up=bf(w_up), ffn_conv_w=ffn_conv_w, ffn_conv_b=ffn_conv_b, w_down=bf(w_down),
    )
    zeros = lambda *shape: jnp.zeros((depth, batch) + shape, F32)
    y_p, (ret_p, hg_p, rgh_p, rgc_p, ffc_p) = _trunk(
        x_prompt, 0, zeros(RET_HEADS, RET_DK, RET_DV), zeros(HG_HEADS, HG_DK, HG_DV), zeros(RG_WIDTH),
        zeros(RG_CONV - 1, RG_WIDTH), zeros(FFN_CONV - 1, D_FF), p)
    y_s, (ret_s, hg_s, rgh_s, rgc_s, ffc_s) = _trunk(
        x_sample, PAST_LEN, state_ret, state_hgrn, state_rglru, cache_rg_conv, cache_ffn_conv, p)
    return (y_p, y_s, ret_p, ret_s, hg_p, hg_s, rgh_p, rgh_s, rgc_p, rgc_s, ffc_p, ffc_s)
```

```python
import functools
import math

import jax
import jax.numpy as jnp
from jax import lax
from jax.experimental import pallas as pl
from jax.experimental.pallas import tpu as pltpu

F32 = jnp.float32
BF16 = jnp.bfloat16

D_MODEL = 1024
PAST_LEN = 2048
CHUNK = 64
RET_HEADS, RET_DK, RET_DV = 4, 128, 256
RET_QK = RET_HEADS * RET_DK
RET_V = RET_HEADS * RET_DV
ROPE_BASE = 10000.0
HG_HEADS, HG_DK, HG_DV = 8, 128, 128
HG_K = HG_HEADS * HG_DK
HG_V = HG_HEADS * HG_DV
HG_MIN_F = 1e-6
RG_BLOCKS, RG_BLOCK = 5, 256
RG_WIDTH = RG_BLOCKS * RG_BLOCK
RG_CONV = 4
RG_C = 8.0
D_FF = 2816
FFN_CONV = 3
EPS = 1e-6

RET_COLS = 2 * RET_QK + 2 * RET_V
HG_COLS = 2 * HG_K + 2 * HG_V
RG_COLS = 2 * RG_WIDTH
GATE_COLS = 3 * D_MODEL
OFF_HG = RET_COLS
OFF_RG = OFF_HG + HG_COLS
OFF_GATE = OFF_RG + RG_COLS

VMEM_LIMIT_BYTES = 56 * 1024 * 1024
SUBLANES = 8
LANES = 128
TIME_BLOCK = 256
ROW_BLOCK = 512
GELU_C = math.sqrt(2.0 / math.pi)

LOG_GAMMA = tuple(math.log1p(-(2.0 ** (-5.0 - h))) for h in range(RET_HEADS))


def _cparams(sem):
    return pltpu.CompilerParams(dimension_semantics=sem, vmem_limit_bytes=VMEM_LIMIT_BYTES)


def _resident(shape):
    nd = len(shape)
    return pl.BlockSpec(shape, lambda *_: (0,) * nd, pipeline_mode=pl.Buffered(1))


def _sigmoid_pair(z):
    e = jnp.exp(-jnp.abs(z))
    d = pl.reciprocal(1.0 + e, approx=True)
    ed = e * d
    pos = z >= 0.0
    return jnp.where(pos, d, ed), jnp.where(pos, ed, d)


def _half_tanh_half(x):
    return jnp.tanh(0.5 * x)


def _sigmoid(x):
    return 0.5 * (1.0 + _half_tanh_half(x))


def _silu(x):
    return (0.5 * x) * (1.0 + _half_tanh_half(x))


def _gelu(x):
    inner = x * (GELU_C + (GELU_C * 0.044715) * (x * x))
    return (0.5 * x) * (1.0 + jnp.tanh(inner))


def _rms(x, w):
    ms = jnp.mean(x * x, axis=-1, keepdims=True)
    return x * lax.rsqrt(ms + EPS) * w


def _dot(a, b):
    return jnp.dot(a, b, preferred_element_type=F32)


def _dot_nt(a, b):
    return lax.dot_general(a, b, (((1,), (1,)), ((), ())), preferred_element_type=F32)


def _dot_tn(a, b):
    return lax.dot_general(a, b, (((0,), (0,)), ((), ())), preferred_element_type=F32)


def _norm_kernel(x_ref, w_ref, o_ref):
    o_ref[...] = _rms(x_ref[...], w_ref[...]).astype(o_ref.dtype)


def _norm(x2d, w, tm):
    rows = x2d.shape[0]
    return pl.pallas_call(
        _norm_kernel,
        grid=(rows // tm,),
        in_specs=[pl.BlockSpec((tm, D_MODEL), lambda i: (i, 0)), _resident((1, D_MODEL))],
        out_specs=pl.BlockSpec((tm, D_MODEL), lambda i: (i, 0)),
        out_shape=jax.ShapeDtypeStruct((rows, D_MODEL), BF16),
        compiler_params=_cparams(("parallel",)),
        name="norm",
    )(x2d, w.reshape(1, D_MODEL))


def _ret_kernel(hn_ref, w_ref, cos_ref, sin_ref, r0_ref, o_ref, r_ref, proj_ref, dec_ref, qd_ref, kd_ref,
                *, tb, chunk):
    t = pl.program_id(1)

    @pl.when(t == 0)
    def _():
        r_ref[...] = r0_ref[...]
        row = lax.broadcasted_iota(jnp.int32, (tb, tb), 0)
        col = lax.broadcasted_iota(jnp.int32, (tb, tb), 1)
        dist = jnp.abs(row - col).astype(F32)
        chunk_bits = chunk.bit_length() - 1
        later_chunk = col >= ((row >> chunk_bits) + 1) * chunk
        n = lax.broadcasted_iota(jnp.int32, (tb, RET_DK), 0).astype(F32)
        for h in range(RET_HEADS):
            lg = LOG_GAMMA[h]
            dec_ref[h] = jnp.where(later_chunk, 0.0, jnp.exp(dist * lg))
            qd_ref[h] = jnp.exp(lg * (n + 1.0))
            kd_ref[h] = jnp.exp(lg * (float(tb) - 1.0 - n))

    proj_ref[...] = _dot(hn_ref[0], w_ref[...])
    cos = cos_ref[...]
    sin = sin_ref[...]
    for h in range(RET_HEADS):
        q = proj_ref[:, h * RET_DK:(h + 1) * RET_DK]
        k = proj_ref[:, RET_QK + h * RET_DK:RET_QK + (h + 1) * RET_DK]
        vb = proj_ref[:, 2 * RET_QK + h * RET_DV:2 * RET_QK + (h + 1) * RET_DV].astype(BF16)
        q = q * cos + pltpu.roll(q, RET_DK // 2, 1) * sin
        k = (k * cos + pltpu.roll(k, RET_DK // 2, 1) * sin) * (RET_DK ** -0.5)
        scores = _dot_nt(q.astype(BF16), k.astype(BF16)) * dec_ref[h]
        r = r_ref[0, h]
        o = _dot(scores.astype(BF16), vb) + _dot((q * qd_ref[h]).astype(BF16), r.astype(BF16))
        r_ref[0, h] = math.exp(LOG_GAMMA[h] * tb) * r + _dot_tn((k * kd_ref[h]).astype(BF16), vb)
        o = o * lax.rsqrt(jnp.mean(o * o, axis=-1, keepdims=True) + EPS)
        g = proj_ref[:, 2 * RET_QK + RET_V + h * RET_DV:2 * RET_QK + RET_V + (h + 1) * RET_DV]
        o_ref[0, :, h * RET_DV:(h + 1) * RET_DV] = (o * _silu(g)).astype(o_ref.dtype)


def _retention(hn, w, cos2, sin2, r0, tb, chunk):
    b, t, _ = hn.shape
    nt = t // tb
    return pl.pallas_call(
        functools.partial(_ret_kernel, tb=tb, chunk=chunk),
        grid=(b, nt),
        in_specs=[
            pl.BlockSpec((1, tb, D_MODEL), lambda i, j: (i, j, 0)),
            _resident((D_MODEL, RET_COLS)),
            pl.BlockSpec((tb, RET_DK), lambda i, j: (j, 0)),
            pl.BlockSpec((tb, RET_DK), lambda i, j: (j, 0)),
            pl.BlockSpec((1, RET_HEADS, RET_DK, RET_DV), lambda i, j: (i, 0, 0, 0)),
        ],
        out_specs=[
            pl.BlockSpec((1, tb, RET_V), lambda i, j: (i, j, 0)),
            pl.BlockSpec((1, RET_HEADS, RET_DK, RET_DV), lambda i, j: (i, 0, 0, 0)),
        ],
        out_shape=[
            jax.ShapeDtypeStruct((b, t, RET_V), BF16),
            jax.ShapeDtypeStruct((b, RET_HEADS, RET_DK, RET_DV), F32),
        ],
        scratch_shapes=[
            pltpu.VMEM((tb, RET_COLS), F32),
            pltpu.VMEM((RET_HEADS, tb, tb), F32),
            pltpu.VMEM((RET_HEADS, tb, RET_DK), F32),
            pltpu.VMEM((RET_HEADS, tb, RET_DK), F32),
        ],
        compiler_params=_cparams(("parallel", "arbitrary")),
        name="retention",
    )(hn, w, cos2, sin2, r0)


def _hg_kernel(hn_ref, w_ref, lb_ref, nw_ref, s0_ref, o_ref, s_ref, proj_ref, st_ref,
               *, chunk, n_chunks, layer, depth):
    t = pl.program_id(1)

    @pl.when(t == 0)
    def _():
        for h in range(HG_HEADS):
            st_ref[h] = s0_ref[0, h].T

    proj_ref[...] = _dot(hn_ref[0], w_ref[...])

    lrows = [lb_ref[d:d + 1, :] for d in range(depth)]
    mx = functools.reduce(jnp.maximum, lrows)
    ex = [jnp.exp(r - mx) for r in lrows]
    den = functools.reduce(lambda a, c: a + c, ex)
    lb = jnp.zeros_like(mx)
    for d in range(1, layer + 1):
        lb = lb + ex[d] / den
    one_m_lb = 1.0 - lb

    row = lax.broadcasted_iota(jnp.int32, (chunk, chunk), 0)
    col = lax.broadcasted_iota(jnp.int32, (chunk, chunk), 1)
    causal = row >= col
    tri = jnp.where(causal, 1.0, 0.0).astype(BF16)
    nw = nw_ref[...]

    def prologue(c):
        rows = slice(c * chunk, (c + 1) * chunk)
        sg, sgn = _sigmoid_pair(proj_ref[rows, HG_K:2 * HG_K])
        if layer == 0:
            fg, kk = sg, sgn
        else:
            fg, kk = lb + one_m_lb * sg, one_m_lb * sgn
        logf = jnp.log(jnp.maximum(fg, HG_MIN_F))
        g_hi = logf.astype(BF16)
        g_lo = (logf - g_hi.astype(F32)).astype(BF16)
        bc = _dot(tri, g_hi) + _dot(tri, g_lo)
        mid = 0.5 * bc[chunk - 1:chunk, :]
        bm = bc - mid
        qt = (_silu(proj_ref[rows, 0:HG_K]) * jnp.exp(bm)).astype(BF16)
        kt = (kk * jnp.exp(-bm)).astype(BF16)
        return dict(rows=rows, emid=jnp.exp(mid), qt=qt, kt=kt)

    def head_products(pro, h):
        sl = slice(h * HG_DK, (h + 1) * HG_DK)
        rows = pro["rows"]
        vb = proj_ref[rows, 2 * HG_K + h * HG_DV:2 * HG_K + (h + 1) * HG_DV].astype(BF16)
        qt, kt = pro["qt"][:, sl], pro["kt"][:, sl]
        return dict(vb=vb, qt=qt, emid=pro["emid"][:, sl], rows=rows, h=h,
                    scores=_dot_nt(qt, kt), kv=_dot_tn(vb, kt))

    def head_finish(hp):
        h, rows, emid = hp["h"], hp["rows"], hp["emid"]
        attn = jnp.where(causal, hp["scores"], 0.0)
        st_e = st_ref[h] * emid
        o = _dot(attn.astype(BF16), hp["vb"]) + _dot_nt(hp["qt"], st_e.astype(BF16))
        st_ref[h] = (st_e + hp["kv"]) * emid
        o = o * lax.rsqrt(jnp.mean(o * o, axis=-1, keepdims=True) + EPS)
        gate = proj_ref[rows, 2 * HG_K + HG_V + h * HG_DV:2 * HG_K + HG_V + (h + 1) * HG_DV]
        o_ref[0, rows, h * HG_DV:(h + 1) * HG_DV] = (
            o * nw[:, h * HG_DV:(h + 1) * HG_DV] * _silu(gate)).astype(o_ref.dtype)

    pro = prologue(0)
    for c in range(n_chunks):
        products = [head_products(pro, h) for h in range(HG_HEADS)]
        if c + 1 < n_chunks:
            pro = prologue(c + 1)
        for hp in products:
            head_finish(hp)

    @pl.when(t == pl.num_programs(1) - 1)
    def _():
        for h in range(HG_HEADS):
            s_ref[0, h] = st_ref[h].T


def _hgrn(hn, w, hg_lb, norm_w, s0, tb, chunk, layer):
    b, t, _ = hn.shape
    nt = t // tb
    depth = hg_lb.shape[0]
    return pl.pallas_call(
        functools.partial(_hg_kernel, chunk=chunk, n_chunks=tb // chunk, layer=layer, depth=depth),
        grid=(b, nt),
        in_specs=[
            pl.BlockSpec((1, tb, D_MODEL), lambda i, j: (i, j, 0)),
            _resident((D_MODEL, HG_COLS)),
            _resident((depth, HG_K)),
            _resident((1, HG_V)),
            pl.BlockSpec((1, HG_HEADS, HG_DK, HG_DV), lambda i, j: (i, 0, 0, 0)),
        ],
        out_specs=[
            pl.BlockSpec((1, tb, HG_V), lambda i, j: (i, j, 0)),
            pl.BlockSpec((1, HG_HEADS, HG_DK, HG_DV), lambda i, j: (i, 0, 0, 0)),
        ],
        out_shape=[
            jax.ShapeDtypeStruct((b, t, HG_V), BF16),
            jax.ShapeDtypeStruct((b, HG_HEADS, HG_DK, HG_DV), F32),
        ],
        scratch_shapes=[pltpu.VMEM((tb, HG_COLS), F32), pltpu.VMEM((HG_HEADS, HG_DV, HG_DK), F32)],
        compiler_params=_cparams(("parallel", "arbitrary")),
        name="hgrn2",
    )(hn, w, hg_lb, norm_w.reshape(1, HG_V), s0)


def _rg_kernel(hn_ref, w_ref, cw_ref, cb_ref, wr_ref, br_ref, wi_ref, bi_ref, lam_ref, h0_ref, buf0_ref,
               o_ref, h_out_ref, buf_out_ref, proj_ref, ubuf_ref, a_ref, b_ref, perm_ref, tail_ref, cin_ref, h_ref,
               *, tb, first_pos_zero, interleaved):
    t = pl.program_id(1)
    seg = tb // SUBLANES
    step = SUBLANES if interleaved else 1
    hist = -(-(RG_CONV - 1) * step // SUBLANES) * SUBLANES
    n_hist = RG_CONV - 1

    @pl.when(t == 0)
    def _():
        tail_ref[...] = buf0_ref[0]
        h_ref[...] = h0_ref[0]
        if interleaved:
            row = lax.broadcasted_iota(jnp.int32, (tb, tb), 0)
            col = lax.broadcasted_iota(jnp.int32, (tb, tb), 1)
            frame_of = lambda p: jnp.bitwise_and(p, SUBLANES - 1) * seg + (p >> (SUBLANES.bit_length() - 1))
            perm_ref[0] = jnp.where(col == frame_of(row), 1.0, 0.0).astype(BF16)
            perm_ref[1] = jnp.where(row == frame_of(col), 1.0, 0.0).astype(BF16)

    hn = hn_ref[0]
    if interleaved:
        hn = _dot(perm_ref[0], hn).astype(BF16)
    proj_ref[...] = _dot(hn, w_ref[...])
    ubuf_ref[hist:hist + tb, :] = proj_ref[:, 0:RG_WIDTH]
    if interleaved:
        row0 = lax.broadcasted_iota(jnp.int32, (SUBLANES, RG_WIDTH), 0) == 0
        for k in range(n_hist):
            src = hist + (seg - n_hist + k) * SUBLANES
            moved = pltpu.roll(ubuf_ref[src:src + SUBLANES, :], 1, 0)
            ubuf_ref[k * SUBLANES:(k + 1) * SUBLANES, :] = jnp.where(row0, tail_ref[k:k + 1, :], moved)
        for k in range(n_hist):
            last = hist + (seg - n_hist + k) * SUBLANES + SUBLANES - 1
            tail_ref[k:k + 1, :] = ubuf_ref[last:last + 1, :]
    else:
        ubuf_ref[hist - n_hist:hist, :] = tail_ref[...]
        tail_ref[...] = ubuf_ref[hist + tb - n_hist:hist + tb, :]

    lam = lam_ref[...]
    neg_half_c_sp = (-0.5 * RG_C) * (jnp.maximum(-lam, 0.0) + jnp.log1p(jnp.exp(-jnp.abs(lam))))
    first_row = lax.broadcasted_iota(jnp.int32, (tb, RG_BLOCK), 0) == 0

    for nb in range(RG_BLOCKS):
        sl = slice(nb * RG_BLOCK, (nb + 1) * RG_BLOCK)
        xc = cb_ref[:, sl]
        for j in range(RG_CONV):
            lo = hist - (n_hist - j) * step
            xc = xc + cw_ref[j:j + 1, sl] * ubuf_ref[lo:lo + tb, sl]
        xcb = xc.astype(BF16)
        tr = _half_tanh_half(_dot(xcb, wr_ref[nb]) + br_ref[:, sl])
        ti = _half_tanh_half(_dot(xcb, wi_ref[nb]) + bi_ref[:, sl])
        log_a = neg_half_c_sp[:, sl] * (1.0 + tr)
        th = jnp.tanh(log_a)
        m2 = -2.0 * th * pl.reciprocal(1.0 - th, approx=True)
        mult = jnp.where(m2 > 0.0, m2 * lax.rsqrt(m2), 0.0)
        if first_pos_zero:
            mult = jnp.where(jnp.logical_and(first_row, t == 0), 1.0, mult)
        a_ref[:, sl] = jnp.exp(log_a)
        b_ref[:, sl] = mult * ((0.5 * xc) * (1.0 + ti))

    if interleaved:
        def tile_scan(i, c):
            hloc, prod = c
            rows = pl.ds(pl.multiple_of(i * SUBLANES, SUBLANES), SUBLANES)
            a = a_ref[rows, :]
            hloc = a * hloc + b_ref[rows, :]
            prod = a * prod
            b_ref[rows, :] = hloc
            a_ref[rows, :] = prod
            return hloc, prod

        init = (jnp.zeros((SUBLANES, RG_WIDTH), F32), jnp.ones((SUBLANES, RG_WIDTH), F32))
        hloc, prod = lax.fori_loop(0, seg, tile_scan, init, unroll=4)
        carry = h_ref[...]
        for s in range(SUBLANES):
            cin_ref[s:s + 1, :] = carry
            carry = hloc[s:s + 1, :] + prod[s:s + 1, :] * carry
        h_ref[...] = carry
        cin = cin_ref[...]

        def tile_out(i, _):
            rows = pl.ds(pl.multiple_of(i * SUBLANES, SUBLANES), SUBLANES)
            h = b_ref[rows, :] + a_ref[rows, :] * cin
            b_ref[rows, :] = h * _gelu(proj_ref[rows, RG_WIDTH:2 * RG_WIDTH])
            return 0

        lax.fori_loop(0, seg, tile_out, 0, unroll=4)
        o_ref[0] = _dot(perm_ref[1], b_ref[...].astype(BF16)).astype(o_ref.dtype)
    else:
        def step_scan(i, h):
            h = a_ref[pl.ds(i, 1), :] * h + b_ref[pl.ds(i, 1), :]
            b_ref[pl.ds(i, 1), :] = h
            return h

        h_ref[...] = lax.fori_loop(0, tb, step_scan, h_ref[...], unroll=8)
        o_ref[0] = (b_ref[...] * _gelu(proj_ref[:, RG_WIDTH:2 * RG_WIDTH])).astype(o_ref.dtype)

    @pl.when(t == pl.num_programs(1) - 1)
    def _():
        h_out_ref[0] = h_ref[...]
        buf_out_ref[0] = tail_ref[...]


def _rglru(hn, w, cw, cb, wr, br, wi, bi, lam, h0, buf0, tb, pos0):
    b, t, _ = hn.shape
    nt = t // tb
    interleaved = tb % LANES == 0
    vec = lambda a: a.reshape(1, RG_WIDTH)
    return pl.pallas_call(
        functools.partial(_rg_kernel, tb=tb, first_pos_zero=(pos0 == 0), interleaved=interleaved),
        grid=(b, nt),
        in_specs=[
            pl.BlockSpec((1, tb, D_MODEL), lambda i, j: (i, j, 0)),
            _resident((D_MODEL, RG_COLS)),
            _resident((RG_CONV, RG_WIDTH)),
            _resident((1, RG_WIDTH)),
            _resident((RG_BLOCKS, RG_BLOCK, RG_BLOCK)),
            _resident((1, RG_WIDTH)),
            _resident((RG_BLOCKS, RG_BLOCK, RG_BLOCK)),
            _resident((1, RG_WIDTH)),
            _resident((1, RG_WIDTH)),
            pl.BlockSpec((1, 1, RG_WIDTH), lambda i, j: (i, 0, 0)),
            pl.BlockSpec((1, RG_CONV - 1, RG_WIDTH), lambda i, j: (i, 0, 0)),
        ],
        out_specs=[
            pl.BlockSpec((1, tb, RG_WIDTH), lambda i, j: (i, j, 0)),
            pl.BlockSpec((1, 1, RG_WIDTH), lambda i, j: (i, 0, 0)),
            pl.BlockSpec((1, RG_CONV - 1, RG_WIDTH), lambda i, j: (i, 0, 0)),
        ],
        out_shape=[
            jax.ShapeDtypeStruct((b, t, RG_WIDTH), BF16),
            jax.ShapeDtypeStruct((b, 1, RG_WIDTH), F32),
            jax.ShapeDtypeStruct((b, RG_CONV - 1, RG_WIDTH), F32),
        ],
        scratch_shapes=[
            pltpu.VMEM((tb, RG_COLS), F32),
            pltpu.VMEM((tb + (RG_CONV - 1) * SUBLANES, RG_WIDTH), F32),
            pltpu.VMEM((tb, RG_WIDTH), F32),
            pltpu.VMEM((tb, RG_WIDTH), F32),
            pltpu.VMEM((2, tb, tb), BF16),
            pltpu.VMEM((RG_CONV - 1, RG_WIDTH), F32),
            pltpu.VMEM((SUBLANES, RG_WIDTH), F32),
            pltpu.VMEM((1, RG_WIDTH), F32),
        ],
        compiler_params=_cparams(("parallel", "arbitrary")),
        name="rglru",
    )(hn, w, cw, vec(cb), wr, vec(br), wi, vec(bi), vec(lam), h0.reshape(b, 1, RG_WIDTH), buf0)


def _merge_kernel(hn_ref, oret_ref, ohg_ref, org_ref, x_ref, wg_ref, wb_ref, wo_ref, out_ref):
    hn = hn_ref[...]
    mixed = _sigmoid(_dot(hn, wg_ref[:, 0:D_MODEL])) * _dot(oret_ref[...], wb_ref[0:RET_V, :])
    mixed = mixed + _sigmoid(_dot(hn, wg_ref[:, D_MODEL:2 * D_MODEL])) * _dot(
        ohg_ref[...], wb_ref[RET_V:RET_V + HG_V, :])
    mixed = mixed + _sigmoid(_dot(hn, wg_ref[:, 2 * D_MODEL:3 * D_MODEL])) * _dot(
        org_ref[...], wb_ref[RET_V + HG_V:RET_V + HG_V + RG_WIDTH, :])
    out_ref[...] = x_ref[...] + _dot(mixed.astype(BF16), wo_ref[...])


def _merge(hn, o_ret, o_hg, o_rg, x, wg, wb, wo, tm):
    rows = x.shape[0]
    rowspec = lambda width: pl.BlockSpec((tm, width), lambda i: (i, 0))
    return pl.pallas_call(
        _merge_kernel,
        grid=(rows // tm,),
        in_specs=[rowspec(D_MODEL), rowspec(RET_V), rowspec(HG_V), rowspec(RG_WIDTH), rowspec(D_MODEL),
                  _resident((D_MODEL, GATE_COLS)), _resident((RET_V + HG_V + RG_WIDTH, D_MODEL)),
                  _resident((D_MODEL, D_MODEL))],
        out_specs=rowspec(D_MODEL),
        out_shape=jax.ShapeDtypeStruct((rows, D_MODEL), F32),
        compiler_params=_cparams(("parallel",)),
        name="merge",
    )(hn, o_ret, o_hg, o_rg, x, wg, wb, wo)


def _ffn_kernel(x_ref, n2_ref, wup_ref, cw_ref, cb_ref, wdn_ref, buf0_ref, fn_ref, out_ref, buf_out_ref,
                abuf_ref, *, tb, final_norm):
    t = pl.program_id(1)
    hist = SUBLANES

    @pl.when(t == 0)
    def _():
        abuf_ref[0:hist, :] = jnp.zeros((hist, D_FF), F32)
        abuf_ref[hist - (FFN_CONV - 1):hist, :] = buf0_ref[0]

    x = x_ref[0]
    hn = _rms(x, n2_ref[...]).astype(BF16)
    abuf_ref[hist:hist + tb, :] = _dot(hn, wup_ref[:, 0:D_FF])
    gate = _dot(hn, wup_ref[:, D_FF:2 * D_FF])
    a = cb_ref[...]
    for j in range(FFN_CONV):
        a = a + cw_ref[j:j + 1, :] * abuf_ref[hist - (FFN_CONV - 1) + j:hist - (FFN_CONV - 1) + j + tb, :]
    y = x + _dot((_gelu(a) * gate).astype(BF16), wdn_ref[...])
    if final_norm:
        y = _rms(y, fn_ref[...])
    out_ref[0] = y

    @pl.when(t == pl.num_programs(1) - 1)
    def _():
        buf_out_ref[0] = abuf_ref[tb + hist - (FFN_CONV - 1):tb + hist, :]

    abuf_ref[0:hist, :] = abuf_ref[tb:tb + hist, :]


def _convffn(x, n2, wup, cw, cb, wdn, buf0, fn, tb, final_norm):
    b, t, _ = x.shape
    nt = t // tb
    return pl.pallas_call(
        functools.partial(_ffn_kernel, tb=tb, final_norm=final_norm),
        grid=(b, nt),
        in_specs=[
            pl.BlockSpec((1, tb, D_MODEL), lambda i, j: (i, j, 0)),
            _resident((1, D_MODEL)),
            _resident((D_MODEL, 2 * D_FF)),
            _resident((FFN_CONV, D_FF)),
            _resident((1, D_FF)),
            _resident((D_FF, D_MODEL)),
            pl.BlockSpec((1, FFN_CONV - 1, D_FF), lambda i, j: (i, 0, 0)),
            _resident((1, D_MODEL)),
        ],
        out_specs=[
            pl.BlockSpec((1, tb, D_MODEL), lambda i, j: (i, j, 0)),
            pl.BlockSpec((1, FFN_CONV - 1, D_FF), lambda i, j: (i, 0, 0)),
        ],
        out_shape=[
            jax.ShapeDtypeStruct((b, t, D_MODEL), F32),
            jax.ShapeDtypeStruct((b, FFN_CONV - 1, D_FF), F32),
        ],
        scratch_shapes=[pltpu.VMEM((tb + SUBLANES, D_FF), F32)],
        compiler_params=_cparams(("parallel", "arbitrary")),
        name="convffn",
    )(x, n2.reshape(1, D_MODEL), wup, cw, cb.reshape(1, D_FF), wdn, buf0, fn.reshape(1, D_MODEL))


def _rope_tables(t, pos0):
    half = RET_DK // 2
    inv = jnp.power(ROPE_BASE, -jnp.arange(half, dtype=F32) / half)
    ang = (jnp.arange(t, dtype=F32) + pos0)[:, None] * inv[None, :]
    cos, sin = jnp.cos(ang), jnp.sin(ang)
    return jnp.concatenate([cos, cos], axis=-1), jnp.concatenate([-sin, sin], axis=-1)


def _block_sizes(t):
    if t <= CHUNK:
        return t, t
    return TIME_BLOCK, CHUNK


def _trunk(x, pos0, r0, s0, h0, rgb0, ffb0, p):
    b, t, _ = x.shape
    tb, chunk = _block_sizes(t)
    rows = b * t
    tm = min(rows, ROW_BLOCK)
    cos2, sin2 = _rope_tables(t, pos0)
    depth = p["norm1_w"].shape[0]
    states = []
    x2d = x.reshape(rows, D_MODEL)
    for l in range(depth):
        hn2d = _norm(x2d, p["norm1_w"][l], tm)
        hn = hn2d.reshape(b, t, D_MODEL)
        o_ret, r_new = _retention(hn, p["w_ret"][l], cos2, sin2, r0[l], tb, chunk)
        o_hg, s_new = _hgrn(hn, p["w_hg"][l], p["hg_lb"], p["hg_norm_w"][l], s0[l], tb, chunk, l)
        o_rg, h_new, rgb_new = _rglru(
            hn, p["w_rg"][l], p["rg_conv_w"][l], p["rg_conv_b"][l], p["rg_w_r"][l], p["rg_b_r"][l],
            p["rg_w_i"][l], p["rg_b_i"][l], p["rg_lambda"][l], h0[l], rgb0[l], tb, pos0)
        x2d = _merge(hn2d, o_ret.reshape(rows, RET_V), o_hg.reshape(rows, HG_V), o_rg.reshape(rows, RG_WIDTH),
                     x2d, p["w_gate"][l], p["w_branch"][l], p["w_out"][l], tm)
        x3d, ffb_new = _convffn(
            x2d.reshape(b, t, D_MODEL), p["norm2_w"][l], p["w_up"][l], p["ffn_conv_w"][l], p["ffn_conv_b"][l],
            p["w_down"][l], ffb0[l], p["final_norm_w"], tb, l == depth - 1)
        x2d = x3d.reshape(rows, D_MODEL)
        states.append((r_new, s_new, h_new.reshape(b, RG_WIDTH), rgb_new, ffb_new))
    new_states = tuple(jnp.stack(st, axis=0) for st in zip(*states))
    return x2d.reshape(b, t, D_MODEL), new_states


def kernel(x_prompt, x_sample, state_ret, state_hgrn, state_rglru, cache_rg_conv, cache_ffn_conv,
           norm1_w, w_in, w_branch, w_out, rg_conv_w, rg_conv_b, rg_w_r, rg_b_r, rg_w_i, rg_b_i,
           rg_lambda, hg_lb, hg_norm_w, norm2_w, w_up, ffn_conv_w, ffn_conv_b, w_down, final_norm_w):
    depth = w_in.shape[0]
    batch = x_prompt.shape[0]
    bf = lambda a: a.astype(BF16)
    p = dict(
        norm1_w=norm1_w, norm2_w=norm2_w, final_norm_w=final_norm_w, hg_lb=hg_lb, hg_norm_w=hg_norm_w,
        w_ret=bf(w_in[:, :, 0:OFF_HG]), w_hg=bf(w_in[:, :, OFF_HG:OFF_RG]),
        w_rg=bf(w_in[:, :, OFF_RG:OFF_GATE]), w_gate=bf(w_in[:, :, OFF_GATE:OFF_GATE + GATE_COLS]),
        w_branch=bf(w_branch), w_out=bf(w_out), rg_conv_w=rg_conv_w, rg_conv_b=rg_conv_b,
        rg_w_r=bf(rg_w_r), rg_b_r=rg_b_r, rg_w_i=bf(rg_w_i), rg_b_i=rg_b_i, rg_lambda=rg_lambda,
        w_up=bf(w_up), ffn_conv_w=ffn_conv_w, ffn_conv_b=ffn_conv_b, w_down=bf(w_down),
    )
    zeros = lambda *shape: jnp.zeros((depth, batch) + shape, F32)
    y_p, (ret_p, hg_p, rgh_p, rgc_p, ffc_p) = _trunk(
        x_prompt, 0, zeros(RET_HEADS, RET_DK, RET_DV), zeros(HG_HEADS, HG_DK, HG_DV), zeros(RG_WIDTH),
        zeros(RG_CONV - 1, RG_WIDTH), zeros(FFN_CONV - 1, D_FF), p)
    y_s, (ret_s, hg_s, rgh_s, rgc_s, ffc_s) = _trunk(
        x_sample, PAST_LEN, state_ret, state_hgrn, state_rglru, cache_rg_conv, cache_ffn_conv, p)
    return (y_p, y_s, ret_p, ret_s, hg_p, hg_s, rgh_p, rgh_s, rgc_p, rgc_s, ffc_p, ffc_s)
```

```python
import functools
import math

import jax
import jax.numpy as jnp
from jax import lax
from jax.experimental import pallas as pl
from jax.experimental.pallas import tpu as pltpu

F32 = jnp.float32
BF16 = jnp.bfloat16

D_MODEL = 1024
PAST_LEN = 2048
CHUNK = 64
RET_HEADS, RET_DK, RET_DV = 4, 128, 256
RET_QK = RET_HEADS * RET_DK
RET_V = RET_HEADS * RET_DV
ROPE_BASE = 10000.0
HG_HEADS, HG_DK, HG_DV = 8, 128, 128
HG_K = HG_HEADS * HG_DK
HG_V = HG_HEADS * HG_DV
HG_MIN_F = 1e-6
HG_SAFE_DECAY = 150.0
RG_BLOCKS, RG_BLOCK = 5, 256
RG_WIDTH = RG_BLOCKS * RG_BLOCK
RG_CONV = 4
RG_C = 8.0
D_FF = 2816
FFN_CONV = 3
EPS = 1e-6

RET_COLS = 2 * RET_QK + 2 * RET_V
HG_COLS = 2 * HG_K + 2 * HG_V
RG_COLS = 2 * RG_WIDTH
GATE_COLS = 3 * D_MODEL
OFF_HG = RET_COLS
OFF_RG = OFF_HG + HG_COLS
OFF_GATE = OFF_RG + RG_COLS

VMEM_LIMIT_BYTES = 56 * 1024 * 1024
SUBLANES = 8
LANES = 128
TIME_BLOCK = 256
ROW_BLOCK = 512
GELU_C = math.sqrt(2.0 / math.pi)

LOG_GAMMA = tuple(math.log1p(-(2.0 ** (-5.0 - h))) for h in range(RET_HEADS))


def _cparams(sem):
    return pltpu.CompilerParams(dimension_semantics=sem, vmem_limit_bytes=VMEM_LIMIT_BYTES)


def _resident(shape):
    nd = len(shape)
    return pl.BlockSpec(shape, lambda *_: (0,) * nd, pipeline_mode=pl.Buffered(1))


def _sigmoid_pair(z):
    e = jnp.exp(-jnp.abs(z))
    d = pl.reciprocal(1.0 + e, approx=True)
    ed = e * d
    pos = z >= 0.0
    return jnp.where(pos, d, ed), jnp.where(pos, ed, d)


def _half_tanh_half(x):
    return jnp.tanh(0.5 * x)


def _sigmoid(x):
    return 0.5 * (1.0 + _half_tanh_half(x))


def _silu(x):
    return (0.5 * x) * (1.0 + _half_tanh_half(x))


def _gelu(x):
    inner = x * (GELU_C + (GELU_C * 0.044715) * (x * x))
    return (0.5 * x) * (1.0 + jnp.tanh(inner))


def _rms(x, w):
    ms = jnp.mean(x * x, axis=-1, keepdims=True)
    return x * lax.rsqrt(ms + EPS) * w


def _dot(a, b):
    return jnp.dot(a, b, preferred_element_type=F32)


def _dot_nt(a, b):
    return lax.dot_general(a, b, (((1,), (1,)), ((), ())), preferred_element_type=F32)


def _dot_tn(a, b):
    return lax.dot_general(a, b, (((0,), (0,)), ((), ())), preferred_element_type=F32)


def _norm_kernel(x_ref, w_ref, o_ref):
    o_ref[...] = _rms(x_ref[...], w_ref[...]).astype(o_ref.dtype)


def _norm(x2d, w, tm):
    rows = x2d.shape[0]
    return pl.pallas_call(
        _norm_kernel,
        grid=(rows // tm,),
        in_specs=[pl.BlockSpec((tm, D_MODEL), lambda i: (i, 0)), _resident((1, D_MODEL))],
        out_specs=pl.BlockSpec((tm, D_MODEL), lambda i: (i, 0)),
        out_shape=jax.ShapeDtypeStruct((rows, D_MODEL), BF16),
        compiler_params=_cparams(("parallel",)),
        name="norm",
    )(x2d, w.reshape(1, D_MODEL))


def _ret_kernel(hn_ref, w_ref, cos_ref, sin_ref, r0_ref, o_ref, r_ref, proj_ref, dec_ref, qd_ref, kd_ref,
                *, tb, chunk):
    t = pl.program_id(1)

    @pl.when(t == 0)
    def _():
        r_ref[...] = r0_ref[...]
        row = lax.broadcasted_iota(jnp.int32, (tb, tb), 0)
        col = lax.broadcasted_iota(jnp.int32, (tb, tb), 1)
        dist = jnp.abs(row - col).astype(F32)
        chunk_bits = chunk.bit_length() - 1
        later_chunk = col >= ((row >> chunk_bits) + 1) * chunk
        n = lax.broadcasted_iota(jnp.int32, (tb, RET_DK), 0).astype(F32)
        for h in range(RET_HEADS):
            lg = LOG_GAMMA[h]
            dec_ref[h] = jnp.where(later_chunk, 0.0, jnp.exp(dist * lg))
            qd_ref[h] = jnp.exp(lg * (n + 1.0))
            kd_ref[h] = jnp.exp(lg * (float(tb) - 1.0 - n))

    proj_ref[...] = _dot(hn_ref[0], w_ref[...])
    cos = cos_ref[...]
    sin = sin_ref[...]
    for h in range(RET_HEADS):
        q = proj_ref[:, h * RET_DK:(h + 1) * RET_DK]
        k = proj_ref[:, RET_QK + h * RET_DK:RET_QK + (h + 1) * RET_DK]
        vb = proj_ref[:, 2 * RET_QK + h * RET_DV:2 * RET_QK + (h + 1) * RET_DV].astype(BF16)
        q = q * cos + pltpu.roll(q, RET_DK // 2, 1) * sin
        k = (k * cos + pltpu.roll(k, RET_DK // 2, 1) * sin) * (RET_DK ** -0.5)
        scores = _dot_nt(q.astype(BF16), k.astype(BF16)) * dec_ref[h]
        r = r_ref[0, h]
        o = _dot(scores.astype(BF16), vb) + _dot((q * qd_ref[h]).astype(BF16), r.astype(BF16))
        r_ref[0, h] = math.exp(LOG_GAMMA[h] * tb) * r + _dot_tn((k * kd_ref[h]).astype(BF16), vb)
        o = o * lax.rsqrt(jnp.mean(o * o, axis=-1, keepdims=True) + EPS)
        g = proj_ref[:, 2 * RET_QK + RET_V + h * RET_DV:2 * RET_QK + RET_V + (h + 1) * RET_DV]
        o_ref[0, :, h * RET_DV:(h + 1) * RET_DV] = (o * _silu(g)).astype(o_ref.dtype)


def _retention(hn, w, cos2, sin2, r0, tb, chunk):
    b, t, _ = hn.shape
    nt = t // tb
    return pl.pallas_call(
        functools.partial(_ret_kernel, tb=tb, chunk=chunk),
        grid=(b, nt),
        in_specs=[
            pl.BlockSpec((1, tb, D_MODEL), lambda i, j: (i, j, 0)),
            _resident((D_MODEL, RET_COLS)),
            pl.BlockSpec((tb, RET_DK), lambda i, j: (j, 0)),
            pl.BlockSpec((tb, RET_DK), lambda i, j: (j, 0)),
            pl.BlockSpec((1, RET_HEADS, RET_DK, RET_DV), lambda i, j: (i, 0, 0, 0)),
        ],
        out_specs=[
            pl.BlockSpec((1, tb, RET_V), lambda i, j: (i, j, 0)),
            pl.BlockSpec((1, RET_HEADS, RET_DK, RET_DV), lambda i, j: (i, 0, 0, 0)),
        ],
        out_shape=[
            jax.ShapeDtypeStruct((b, t, RET_V), BF16),
            jax.ShapeDtypeStruct((b, RET_HEADS, RET_DK, RET_DV), F32),
        ],
        scratch_shapes=[
            pltpu.VMEM((tb, RET_COLS), F32),
            pltpu.VMEM((RET_HEADS, tb, tb), F32),
            pltpu.VMEM((RET_HEADS, tb, RET_DK), F32),
            pltpu.VMEM((RET_HEADS, tb, RET_DK), F32),
        ],
        compiler_params=_cparams(("parallel", "arbitrary")),
        name="retention",
    )(hn, w, cos2, sin2, r0)


def _hg_kernel(hn_ref, w_ref, lb_ref, nw_ref, s0_ref, o_ref, s_ref, proj_ref, st_ref, lf_ref, kk_ref,
               bc_ref, qs_ref, acc_ref, *, chunk, n_chunks, layer, depth):
    t = pl.program_id(1)

    @pl.when(t == 0)
    def _():
        for h in range(HG_HEADS):
            st_ref[h] = s0_ref[0, h].T

    def project(lo, hi):
        proj_ref[:, lo:hi] = _dot(hn_ref[0], w_ref[:, lo:hi])

    project(HG_K, 2 * HG_K)
    project(0, HG_K)

    lrows = [lb_ref[d:d + 1, :] for d in range(depth)]
    mx = functools.reduce(jnp.maximum, lrows)
    ex = [jnp.exp(r - mx) for r in lrows]
    den = functools.reduce(lambda a, c: a + c, ex)
    lb = jnp.zeros_like(mx)
    for d in range(1, layer + 1):
        lb = lb + ex[d] / den
    one_m_lb = 1.0 - lb

    row = lax.broadcasted_iota(jnp.int32, (chunk, chunk), 0)
    col = lax.broadcasted_iota(jnp.int32, (chunk, chunk), 1)
    causal = row >= col
    tri = jnp.where(causal, 1.0, 0.0).astype(BF16)
    nw = nw_ref[...]

    worst = None
    for c in range(n_chunks):
        rows = slice(c * chunk, (c + 1) * chunk)
        sg, sgn = _sigmoid_pair(proj_ref[rows, HG_K:2 * HG_K])
        if layer == 0:
            fg, kk = sg, sgn
        else:
            fg, kk = lb + one_m_lb * sg, one_m_lb * sgn
        logf = jnp.log(jnp.maximum(fg, HG_MIN_F))
        lf_ref[rows, :] = logf
        kk_ref[rows, :] = kk
        total = jnp.sum(logf, axis=0, keepdims=True)
        worst = total if worst is None else jnp.minimum(worst, total)
    steep = jnp.min(worst) < -HG_SAFE_DECAY

    def cum_log_decay(c):
        logf = lf_ref[c * chunk:(c + 1) * chunk, :]
        g_hi = logf.astype(BF16)
        g_lo = (logf - g_hi.astype(F32)).astype(BF16)
        return _dot(tri, g_hi) + _dot(tri, g_lo)

    def prologue(c):
        rows = slice(c * chunk, (c + 1) * chunk)
        bc = cum_log_decay(c)
        mid = 0.5 * bc[chunk - 1:chunk, :]
        bm = bc - mid
        qt = (_silu(proj_ref[rows, 0:HG_K]) * jnp.exp(bm)).astype(BF16)
        kt = (kk_ref[rows, :] * jnp.exp(-bm)).astype(BF16)
        return dict(rows=rows, emid=jnp.exp(mid), qt=qt, kt=kt)

    def head_products(pro, h):
        sl = slice(h * HG_DK, (h + 1) * HG_DK)
        rows = pro["rows"]
        vb = proj_ref[rows, 2 * HG_K + h * HG_DV:2 * HG_K + (h + 1) * HG_DV].astype(BF16)
        qt, kt = pro["qt"][:, sl], pro["kt"][:, sl]
        return dict(vb=vb, qt=qt, emid=pro["emid"][:, sl], rows=rows, h=h,
                    scores=_dot_nt(qt, kt), kv=_dot_tn(vb, kt))

    def head_finish(hp):
        h, rows, emid = hp["h"], hp["rows"], hp["emid"]
        attn = jnp.where(causal, hp["scores"], 0.0)
        st_e = st_ref[h] * emid
        o = _dot(attn.astype(BF16), hp["vb"]) + _dot_nt(hp["qt"], st_e.astype(BF16))
        st_ref[h] = (st_e + hp["kv"]) * emid
        write_output(o, rows, h)

    def write_output(o, rows, h):
        o = o * lax.rsqrt(jnp.mean(o * o, axis=-1, keepdims=True) + EPS)
        gate = proj_ref[rows, 2 * HG_K + HG_V + h * HG_DV:2 * HG_K + HG_V + (h + 1) * HG_DV]
        o_ref[0, rows, h * HG_DV:(h + 1) * HG_DV] = (
            o * nw[:, h * HG_DV:(h + 1) * HG_DV] * _silu(gate)).astype(o_ref.dtype)

    @pl.when(jnp.logical_not(steep))
    def _():
        pro = prologue(0)
        project(2 * HG_K, 2 * HG_K + HG_V)
        for c in range(n_chunks):
            products = [head_products(pro, h) for h in range(HG_HEADS)]
            if c == 0:
                project(2 * HG_K + HG_V, 2 * HG_K + 2 * HG_V)
            if c + 1 < n_chunks:
                pro = prologue(c + 1)
            for hp in products:
                head_finish(hp)

    @pl.when(steep)
    def _():
        project(2 * HG_K, 2 * HG_K + 2 * HG_V)
        live_rows =lax.broadcasted_iota(jnp.int32, (chunk, HG_DV), 0)
        for c in range(n_chunks):
            rows = slice(c * chunk, (c + 1) * chunk)
            bc = cum_log_decay(c)
            bc_ref[...] = bc
            qs_ref[...] = _silu(proj_ref[rows, 0:HG_K])
            acc_ref[...] = jnp.zeros(acc_ref.shape, F32)

            def source_frame(s, carry):
                frame = pl.ds(c * chunk + s, 1)
                decay = jnp.exp(jnp.minimum(bc_ref[...] - bc_ref[pl.ds(s, 1), :], 0.0))
                term = qs_ref[...] * decay * kk_ref[frame, :]
                v_row = proj_ref[frame, 2 * HG_K:2 * HG_K + HG_V]
                for h in range(HG_HEADS):
                    sl = slice(h * HG_DK, (h + 1) * HG_DK)
                    weight = jnp.sum(term[:, sl], axis=1, keepdims=True)
                    acc_ref[:, sl] += jnp.where(live_rows >= s, weight * v_row[:, sl], 0.0)
                return carry

            lax.fori_loop(0, chunk, source_frame, 0)
            total = bc[chunk - 1:chunk, :]
            q_dec = (qs_ref[...] * jnp.exp(bc)).astype(BF16)
            k_dec = (kk_ref[rows, :] * jnp.exp(total - bc)).astype(BF16)
            s_dec = jnp.exp(total)
            for h in range(HG_HEADS):
                sl = slice(h * HG_DK, (h + 1) * HG_DK)
                vb = proj_ref[rows, 2 * HG_K + h * HG_DV:2 * HG_K + (h + 1) * HG_DV].astype(BF16)
                st = st_ref[h]
                o = acc_ref[:, sl] + _dot_nt(q_dec[:, sl], st.astype(BF16))
                st_ref[h] = st * s_dec[:, sl] + _dot_tn(vb, k_dec[:, sl])
                write_output(o, rows, h)

    @pl.when(t == pl.num_programs(1) - 1)
    def _():
        for h in range(HG_HEADS):
            s_ref[0, h] = st_ref[h].T


def _hgrn(hn, w, hg_lb, norm_w, s0, tb, chunk, layer):
    b, t, _ = hn.shape
    nt = t // tb
    depth = hg_lb.shape[0]
    return pl.pallas_call(
        functools.partial(_hg_kernel, chunk=chunk, n_chunks=tb // chunk, layer=layer, depth=depth),
        grid=(b, nt),
        in_specs=[
            pl.BlockSpec((1, tb, D_MODEL), lambda i, j: (i, j, 0)),
            _resident((D_MODEL, HG_COLS)),
            _resident((depth, HG_K)),
            _resident((1, HG_V)),
            pl.BlockSpec((1, HG_HEADS, HG_DK, HG_DV), lambda i, j: (i, 0, 0, 0)),
        ],
        out_specs=[
            pl.BlockSpec((1, tb, HG_V), lambda i, j: (i, j, 0)),
            pl.BlockSpec((1, HG_HEADS, HG_DK, HG_DV), lambda i, j: (i, 0, 0, 0)),
        ],
        out_shape=[
            jax.ShapeDtypeStruct((b, t, HG_V), BF16),
            jax.ShapeDtypeStruct((b, HG_HEADS, HG_DK, HG_DV), F32),
        ],
        scratch_shapes=[
            pltpu.VMEM((tb, HG_COLS), F32),
            pltpu.VMEM((HG_HEADS, HG_DV, HG_DK), F32),
            pltpu.VMEM((tb, HG_K), F32),
            pltpu.VMEM((tb, HG_K), F32),
            pltpu.VMEM((chunk, HG_K), F32),
            pltpu.VMEM((chunk, HG_K), F32),
            pltpu.VMEM((chunk, HG_V), F32),
        ],
        compiler_params=_cparams(("parallel", "arbitrary")),
        name="hgrn2",
    )(hn, w, hg_lb, norm_w.reshape(1, HG_V), s0)


def _rg_kernel(hn_ref, w_ref, cw_ref, cb_ref, wr_ref, br_ref, wi_ref, bi_ref, lam_ref, h0_ref, buf0_ref,
               o_ref, h_out_ref, buf_out_ref, proj_ref, ubuf_ref, a_ref, b_ref, perm_ref, tail_ref, cin_ref, h_ref,
               *, tb, first_pos_zero, interleaved):
    t = pl.program_id(1)
    seg = tb // SUBLANES
    step = SUBLANES if interleaved else 1
    hist = -(-(RG_CONV - 1) * step // SUBLANES) * SUBLANES
    n_hist = RG_CONV - 1

    @pl.when(t == 0)
    def _():
        tail_ref[...] = buf0_ref[0]
        h_ref[...] = h0_ref[0]
        if interleaved:
            row = lax.broadcasted_iota(jnp.int32, (tb, tb), 0)
            col = lax.broadcasted_iota(jnp.int32, (tb, tb), 1)
            frame_of = lambda p: jnp.bitwise_and(p, SUBLANES - 1) * seg + (p >> (SUBLANES.bit_length() - 1))
            perm_ref[0] = jnp.where(col == frame_of(row), 1.0, 0.0).astype(BF16)
            perm_ref[1] = jnp.where(row == frame_of(col), 1.0, 0.0).astype(BF16)

    hn = hn_ref[0]
    if interleaved:
        hn = _dot(perm_ref[0], hn).astype(BF16)
    proj_ref[...] = _dot(hn, w_ref[...])
    ubuf_ref[hist:hist + tb, :] = proj_ref[:, 0:RG_WIDTH]
    if interleaved:
        row0 = lax.broadcasted_iota(jnp.int32, (SUBLANES, RG_WIDTH), 0) == 0
        for k in range(n_hist):
            src = hist + (seg - n_hist + k) * SUBLANES
            moved = pltpu.roll(ubuf_ref[src:src + SUBLANES, :], 1, 0)
            ubuf_ref[k * SUBLANES:(k + 1) * SUBLANES, :] = jnp.where(row0, tail_ref[k:k + 1, :], moved)
        for k in range(n_hist):
            last = hist + (seg - n_hist + k) * SUBLANES + SUBLANES - 1
            tail_ref[k:k + 1, :] = ubuf_ref[last:last + 1, :]
    else:
        ubuf_ref[hist - n_hist:hist, :] = tail_ref[...]
        tail_ref[...] = ubuf_ref[hist + tb - n_hist:hist + tb, :]

    lam = lam_ref[...]
    neg_half_c_sp = (-0.5 * RG_C) * (jnp.maximum(-lam, 0.0) + jnp.log1p(jnp.exp(-jnp.abs(lam))))
    first_row = lax.broadcasted_iota(jnp.int32, (tb, RG_BLOCK), 0) == 0

    for nb in range(RG_BLOCKS):
        sl = slice(nb * RG_BLOCK, (nb + 1) * RG_BLOCK)
        xc = cb_ref[:, sl]
        for j in range(RG_CONV):
            lo = hist - (n_hist - j) * step
            xc = xc + cw_ref[j:j + 1, sl] * ubuf_ref[lo:lo + tb, sl]
        xcb = xc.astype(BF16)
        tr = _half_tanh_half(_dot(xcb, wr_ref[nb]) + br_ref[:, sl])
        ti = _half_tanh_half(_dot(xcb, wi_ref[nb]) + bi_ref[:, sl])
        log_a = neg_half_c_sp[:, sl] * (1.0 + tr)
        th = jnp.tanh(log_a)
        m2 = -2.0 * th * pl.reciprocal(1.0 - th, approx=True)
        mult = jnp.where(m2 > 0.0, m2 * lax.rsqrt(m2), 0.0)
        if first_pos_zero:
            mult = jnp.where(jnp.logical_and(first_row, t == 0), 1.0, mult)
        a_ref[:, sl] = jnp.exp(log_a)
        b_ref[:, sl] = mult * ((0.5 * xc) * (1.0 + ti))

    if interleaved:
        def tile_scan(i, c):
            hloc, prod = c
            rows = pl.ds(pl.multiple_of(i * SUBLANES, SUBLANES), SUBLANES)
            a = a_ref[rows, :]
            hloc = a * hloc + b_ref[rows, :]
            prod = a * prod
            b_ref[rows, :] = hloc
            a_ref[rows, :] = prod
            return hloc, prod

        init = (jnp.zeros((SUBLANES, RG_WIDTH), F32), jnp.ones((SUBLANES, RG_WIDTH), F32))
        hloc, prod = lax.fori_loop(0, seg, tile_scan, init, unroll=4)
        carry = h_ref[...]
        for s in range(SUBLANES):
            cin_ref[s:s + 1, :] = carry
            carry = hloc[s:s + 1, :] + prod[s:s + 1, :] * carry
        h_ref[...] = carry
        cin = cin_ref[...]

        def tile_out(i, _):
            rows = pl.ds(pl.multiple_of(i * SUBLANES, SUBLANES), SUBLANES)
            h = b_ref[rows, :] + a_ref[rows, :] * cin
            b_ref[rows, :] = h * _gelu(proj_ref[rows, RG_WIDTH:2 * RG_WIDTH])
            return 0

        lax.fori_loop(0, seg, tile_out, 0, unroll=4)
        o_ref[0] = _dot(perm_ref[1], b_ref[...].astype(BF16)).astype(o_ref.dtype)
    else:
        def step_scan(i, h):
            h = a_ref[pl.ds(i, 1), :] * h + b_ref[pl.ds(i, 1), :]
            b_ref[pl.ds(i, 1), :] = h
            return h

        h_ref[...] = lax.fori_loop(0, tb, step_scan, h_ref[...], unroll=8)
        o_ref[0] = (b_ref[...] * _gelu(proj_ref[:, RG_WIDTH:2 * RG_WIDTH])).astype(o_ref.dtype)

    @pl.when(t == pl.num_programs(1) - 1)
    def _():
        h_out_ref[0] = h_ref[...]
        buf_out_ref[0] = tail_ref[...]


def _rglru(hn, w, cw, cb, wr, br, wi, bi, lam, h0, buf0, tb, pos0):
    b, t, _ = hn.shape
    nt = t // tb
    interleaved = tb % LANES == 0
    vec = lambda a: a.reshape(1, RG_WIDTH)
    return pl.pallas_call(
        functools.partial(_rg_kernel, tb=tb, first_pos_zero=(pos0 == 0), interleaved=interleaved),
        grid=(b, nt),
        in_specs=[
            pl.BlockSpec((1, tb, D_MODEL), lambda i, j: (i, j, 0)),
            _resident((D_MODEL, RG_COLS)),
            _resident((RG_CONV, RG_WIDTH)),
            _resident((1, RG_WIDTH)),
            _resident((RG_BLOCKS, RG_BLOCK, RG_BLOCK)),
            _resident((1, RG_WIDTH)),
            _resident((RG_BLOCKS, RG_BLOCK, RG_BLOCK)),
            _resident((1, RG_WIDTH)),
            _resident((1, RG_WIDTH)),
            pl.BlockSpec((1, 1, RG_WIDTH), lambda i, j: (i, 0, 0)),
            pl.BlockSpec((1, RG_CONV - 1, RG_WIDTH), lambda i, j: (i, 0, 0)),
        ],
        out_specs=[
            pl.BlockSpec((1, tb, RG_WIDTH), lambda i, j: (i, j, 0)),
            pl.BlockSpec((1, 1, RG_WIDTH), lambda i, j: (i, 0, 0)),
            pl.BlockSpec((1, RG_CONV - 1, RG_WIDTH), lambda i, j: (i, 0, 0)),
        ],
        out_shape=[
            jax.ShapeDtypeStruct((b, t, RG_WIDTH), BF16),
            jax.ShapeDtypeStruct((b, 1, RG_WIDTH), F32),
            jax.ShapeDtypeStruct((b, RG_CONV - 1, RG_WIDTH), F32),
        ],
        scratch_shapes=[
            pltpu.VMEM((tb, RG_COLS), F32),
            pltpu.VMEM((tb + (RG_CONV - 1) * SUBLANES, RG_WIDTH), F32),
            pltpu.VMEM((tb, RG_WIDTH), F32),
            pltpu.VMEM((tb, RG_WIDTH), F32),
            pltpu.VMEM((2, tb, tb), BF16),
            pltpu.VMEM((RG_CONV - 1, RG_WIDTH), F32),
            pltpu.VMEM((SUBLANES, RG_WIDTH), F32),
            pltpu.VMEM((1, RG_WIDTH), F32),
        ],
        compiler_params=_cparams(("parallel", "arbitrary")),
        name="rglru",
    )(hn, w, cw, vec(cb), wr, vec(br), wi, vec(bi), vec(lam), h0.reshape(b, 1, RG_WIDTH), buf0)


def _merge_kernel(hn_ref, oret_ref, ohg_ref, org_ref, x_ref, wg_ref, wb_ref, wo_ref, out_ref):
    hn = hn_ref[...]
    mixed = _sigmoid(_dot(hn, wg_ref[:, 0:D_MODEL])) * _dot(oret_ref[...], wb_ref[0:RET_V, :])
    mixed = mixed + _sigmoid(_dot(hn, wg_ref[:, D_MODEL:2 * D_MODEL])) * _dot(
        ohg_ref[...], wb_ref[RET_V:RET_V + HG_V, :])
    mixed = mixed + _sigmoid(_dot(hn, wg_ref[:, 2 * D_MODEL:3 * D_MODEL])) * _dot(
        org_ref[...], wb_ref[RET_V + HG_V:RET_V + HG_V + RG_WIDTH, :])
    out_ref[...] = x_ref[...] + _dot(mixed.astype(BF16), wo_ref[...])


def _merge(hn, o_ret, o_hg, o_rg, x, wg, wb, wo, tm):
    rows = x.shape[0]
    rowspec = lambda width: pl.BlockSpec((tm, width), lambda i: (i, 0))
    return pl.pallas_call(
        _merge_kernel,
        grid=(rows // tm,),
        in_specs=[rowspec(D_MODEL), rowspec(RET_V), rowspec(HG_V), rowspec(RG_WIDTH), rowspec(D_MODEL),
                  _resident((D_MODEL, GATE_COLS)), _resident((RET_V + HG_V + RG_WIDTH, D_MODEL)),
                  _resident((D_MODEL, D_MODEL))],
        out_specs=rowspec(D_MODEL),
        out_shape=jax.ShapeDtypeStruct((rows, D_MODEL), F32),
        compiler_params=_cparams(("parallel",)),
        name="merge",
    )(hn, o_ret, o_hg, o_rg, x, wg, wb, wo)


def _ffn_kernel(x_ref, n2_ref, wup_ref, cw_ref, cb_ref, wdn_ref, buf0_ref, fn_ref, out_ref, buf_out_ref,
                abuf_ref, *, tb, final_norm):
    t = pl.program_id(1)
    hist = SUBLANES

    @pl.when(t == 0)
    def _():
        abuf_ref[0:hist, :] = jnp.zeros((hist, D_FF), F32)
        abuf_ref[hist - (FFN_CONV - 1):hist, :] = buf0_ref[0]

    x = x_ref[0]
    hn = _rms(x, n2_ref[...]).astype(BF16)
    abuf_ref[hist:hist + tb, :] = _dot(hn, wup_ref[:, 0:D_FF])
    gate = _dot(hn, wup_ref[:, D_FF:2 * D_FF])
    a = cb_ref[...]
    for j in range(FFN_CONV):
        a = a + cw_ref[j:j + 1, :] * abuf_ref[hist - (FFN_CONV - 1) + j:hist - (FFN_CONV - 1) + j + tb, :]
    y = x + _dot((_gelu(a) * gate).astype(BF16), wdn_ref[...])
    if final_norm:
        y = _rms(y, fn_ref[...])
    out_ref[0] = y

    @pl.when(t == pl.num_programs(1) - 1)
    def _():
        buf_out_ref[0] = abuf_ref[tb + hist - (FFN_CONV - 1):tb + hist, :]

    abuf_ref[0:hist, :] = abuf_ref[tb:tb + hist, :]


def _convffn(x, n2, wup, cw, cb, wdn, buf0, fn, tb, final_norm):
    b, t, _ = x.shape
    nt = t // tb
    return pl.pallas_call(
        functools.partial(_ffn_kernel, tb=tb, final_norm=final_norm),
        grid=(b, nt),
        in_specs=[
            pl.BlockSpec((1, tb, D_MODEL), lambda i, j: (i, j, 0)),
            _resident((1, D_MODEL)),
            _resident((D_MODEL, 2 * D_FF)),
            _resident((FFN_CONV, D_FF)),
            _resident((1, D_FF)),
            _resident((D_FF, D_MODEL)),
            pl.BlockSpec((1, FFN_CONV - 1, D_FF), lambda i, j: (i, 0, 0)),
            _resident((1, D_MODEL)),
        ],
        out_specs=[
            pl.BlockSpec((1, tb, D_MODEL), lambda i, j: (i, j, 0)),
            pl.BlockSpec((1, FFN_CONV - 1, D_FF), lambda i, j: (i, 0, 0)),
        ],
        out_shape=[
            jax.ShapeDtypeStruct((b, t, D_MODEL), F32),
            jax.ShapeDtypeStruct((b, FFN_CONV - 1, D_FF), F32),
        ],
        scratch_shapes=[pltpu.VMEM((tb + SUBLANES, D_FF), F32)],
        compiler_params=_cparams(("parallel", "arbitrary")),
        name="convffn",
    )(x, n2.reshape(1, D_MODEL), wup, cw, cb.reshape(1, D_FF), wdn, buf0, fn.reshape(1, D_MODEL))


def _rope_tables(t, pos0):
    half = RET_DK // 2
    inv = jnp.power(ROPE_BASE, -jnp.arange(half, dtype=F32) / half)
    ang = (jnp.arange(t, dtype=F32) + pos0)[:, None] * inv[None, :]
    cos, sin = jnp.cos(ang), jnp.sin(ang)
    return jnp.concatenate([cos, cos], axis=-1), jnp.concatenate([-sin, sin], axis=-1)


def _block_sizes(t):
    if t <= CHUNK:
        return t, t
    return TIME_BLOCK, CHUNK


def _trunk(x, pos0, r0, s0, h0, rgb0, ffb0, p):
    b, t, _ = x.shape
    tb, chunk = _block_sizes(t)
    rows = b * t
    tm = min(rows, ROW_BLOCK)
    cos2, sin2 = _rope_tables(t, pos0)
    depth = p["norm1_w"].shape[0]
    states = []
    x2d = x.reshape(rows, D_MODEL)
    for l in range(depth):
        hn2d = _norm(x2d, p["norm1_w"][l], tm)
        hn = hn2d.reshape(b, t, D_MODEL)
        o_ret, r_new = _retention(hn, p["w_ret"][l], cos2, sin2, r0[l], tb, chunk)
        o_hg, s_new = _hgrn(hn, p["w_hg"][l], p["hg_lb"], p["hg_norm_w"][l], s0[l], tb, chunk, l)
        o_rg, h_new, rgb_new = _rglru(
            hn, p["w_rg"][l], p["rg_conv_w"][l], p["rg_conv_b"][l], p["rg_w_r"][l], p["rg_b_r"][l],
            p["rg_w_i"][l], p["rg_b_i"][l], p["rg_lambda"][l], h0[l], rgb0[l], tb, pos0)
        x2d = _merge(hn2d, o_ret.reshape(rows, RET_V), o_hg.reshape(rows, HG_V), o_rg.reshape(rows, RG_WIDTH),
                     x2d, p["w_gate"][l], p["w_branch"][l], p["w_out"][l], tm)
        x3d, ffb_new = _convffn(
            x2d.reshape(b, t, D_MODEL), p["norm2_w"][l], p["w_up"][l], p["ffn_conv_w"][l], p["ffn_conv_b"][l],
            p["w_down"][l], ffb0[l], p["final_norm_w"], tb, l == depth - 1)
        x2d = x3d.reshape(rows, D_MODEL)
        states.append((r_new, s_new, h_new.reshape(b, RG_WIDTH), rgb_new, ffb_new))
    new_states = tuple(jnp.stack(st, axis=0) for st in zip(*states))
    return x2d.reshape(b, t, D_MODEL), new_states


def kernel(x_prompt, x_sample, state_ret, state_hgrn, state_rglru, cache_rg_conv, cache_ffn_conv,
           norm1_w, w_in, w_branch, w_out, rg_conv_w, rg_conv_b, rg_w_r, rg_b_r, rg_w_i, rg_b_i,
           rg_lambda, hg_lb, hg_norm_w, norm2_w, w_up, ffn_conv_w, ffn_conv_b, w_down, final_norm_w):
    depth = w_in.shape[0]
    batch = x_prompt.shape[0]
    bf = lambda a: a.astype(BF16)
    p = dict(
        norm1_w=norm1_w, norm2_w=norm2_w, final_norm_w=final_norm_w, hg_lb=hg_lb, hg_norm_w=hg_norm_w,
        w_ret=bf(w_in[:, :, 0:OFF_HG]), w_hg=bf(w_in[:, :, OFF_HG:OFF_RG]),
        w_rg=bf(w_in[:, :, OFF_RG:OFF_GATE]), w_gate=bf(w_in[:, :, OFF_GATE:OFF_GATE + GATE_COLS]),
        w_branch=bf(w_branch), w_out=bf(w_out), rg_conv_w=rg_conv_w, rg_conv_b=rg_conv_b,
        rg_w_r=bf(rg_w_r), rg_b_r=rg_b_r, rg_w_i=bf(rg_w_i), rg_b_i=rg_b_i, rg_lambda=rg_lambda,
        w_up=bf(w_up), ffn_conv_w=ffn_conv_w, ffn_conv_b=ffn_conv_b, w_down=bf(w_down),
    )
    zeros = lambda *shape: jnp.zeros((depth, batch) + shape, F32)
    y_p, (ret_p, hg_p, rgh_p, rgc_p, ffc_p) = _trunk(
        x_prompt, 0, zeros(RET_HEADS, RET_DK, RET_DV), zeros(HG_HEADS, HG_DK, HG_DV), zeros(RG_WIDTH),
        zeros(RG_CONV - 1, RG_WIDTH), zeros(FFN_CONV - 1, D_FF), p)
    y_s, (ret_s, hg_s, rgh_s, rgc_s, ffc_s) = _trunk(
        x_sample, PAST_LEN, state_ret, state_hgrn, state_rglru, cache_rg_conv, cache_ffn_conv, p)
    return (y_p, y_s, ret_p, ret_s, hg_p, hg_s, rgh_p, rgh_s, rgc_p, rgc_s, ffc_p, ffc_s)
```

```python
import functools
import math

import jax
import jax.numpy as jnp
from jax import lax
from jax.experimental import pallas as pl
from jax.experimental.pallas import tpu as pltpu

F32 = jnp.float32
BF16 = jnp.bfloat16

D_MODEL = 1024
PAST_LEN = 2048
CHUNK = 64
RET_HEADS, RET_DK, RET_DV = 4, 128, 256
RET_QK = RET_HEADS * RET_DK
RET_V = RET_HEADS * RET_DV
ROPE_BASE = 10000.0
HG_HEADS, HG_DK, HG_DV = 8, 128, 128
HG_K = HG_HEADS * HG_DK
HG_V = HG_HEADS * HG_DV
HG_MIN_F = 1e-6
HG_SAFE_DECAY = 150.0
RG_BLOCKS, RG_BLOCK = 5, 256
RG_WIDTH = RG_BLOCKS * RG_BLOCK
RG_CONV = 4
RG_C = 8.0
D_FF = 2816
FFN_CONV = 3
EPS = 1e-6

RET_COLS = 2 * RET_QK + 2 * RET_V
HG_COLS = 2 * HG_K + 2 * HG_V
RG_COLS = 2 * RG_WIDTH
GATE_COLS = 3 * D_MODEL
OFF_HG = RET_COLS
OFF_RG = OFF_HG + HG_COLS
OFF_GATE = OFF_RG + RG_COLS

VMEM_LIMIT_BYTES = 56 * 1024 * 1024
SUBLANES = 8
LANES = 128
TIME_BLOCK = 256
ROW_BLOCK = 512
GELU_C = math.sqrt(2.0 / math.pi)

LOG_GAMMA = tuple(math.log1p(-(2.0 ** (-5.0 - h))) for h in range(RET_HEADS))


def _cparams(sem):
    return pltpu.CompilerParams(dimension_semantics=sem, vmem_limit_bytes=VMEM_LIMIT_BYTES)


def _resident(shape):
    nd = len(shape)
    return pl.BlockSpec(shape, lambda *_: (0,) * nd, pipeline_mode=pl.Buffered(1))


def _sigmoid_pair(z):
    e = jnp.exp(-jnp.abs(z))
    d = pl.reciprocal(1.0 + e, approx=True)
    ed = e * d
    pos = z >= 0.0
    return jnp.where(pos, d, ed), jnp.where(pos, ed, d)


def _half_tanh_half(x):
    return jnp.tanh(0.5 * x)


def _sigmoid(x):
    return 0.5 * (1.0 + _half_tanh_half(x))


def _silu(x):
    return (0.5 * x) * (1.0 + _half_tanh_half(x))


def _gelu(x):
    inner = x * (GELU_C + (GELU_C * 0.044715) * (x * x))
    return (0.5 * x) * (1.0 + jnp.tanh(inner))


def _rms(x, w):
    ms = jnp.mean(x * x, axis=-1, keepdims=True)
    return x * lax.rsqrt(ms + EPS) * w


def _dot(a, b):
    return jnp.dot(a, b, preferred_element_type=F32)


def _dot_nt(a, b):
    return lax.dot_general(a, b, (((1,), (1,)), ((), ())), preferred_element_type=F32)


def _dot_tn(a, b):
    return lax.dot_general(a, b, (((0,), (0,)), ((), ())), preferred_element_type=F32)


def _norm_kernel(x_ref, w_ref, o_ref):
    o_ref[...] = _rms(x_ref[...], w_ref[...]).astype(o_ref.dtype)


def _norm(x2d, w, tm):
    rows = x2d.shape[0]
    return pl.pallas_call(
        _norm_kernel,
        grid=(rows // tm,),
        in_specs=[pl.BlockSpec((tm, D_MODEL), lambda i: (i, 0)), _resident((1, D_MODEL))],
        out_specs=pl.BlockSpec((tm, D_MODEL), lambda i: (i, 0)),
        out_shape=jax.ShapeDtypeStruct((rows, D_MODEL), BF16),
        compiler_params=_cparams(("parallel",)),
        name="norm",
    )(x2d, w.reshape(1, D_MODEL))


def _ret_kernel(hn_ref, w_ref, cos_ref, sin_ref, r0_ref, o_ref, r_ref, proj_ref, dec_ref, qd_ref, kd_ref,
                *, tb, chunk):
    t = pl.program_id(1)

    @pl.when(t == 0)
    def _():
        r_ref[...] = r0_ref[...]
        row = lax.broadcasted_iota(jnp.int32, (tb, tb), 0)
        col = lax.broadcasted_iota(jnp.int32, (tb, tb), 1)
        dist = jnp.abs(row - col).astype(F32)
        chunk_bits = chunk.bit_length() - 1
        later_chunk = col >= ((row >> chunk_bits) + 1) * chunk
        n = lax.broadcasted_iota(jnp.int32, (tb, RET_DK), 0).astype(F32)
        for h in range(RET_HEADS):
            lg = LOG_GAMMA[h]
            dec_ref[h] = jnp.where(later_chunk, 0.0, jnp.exp(dist * lg))
            qd_ref[h] = jnp.exp(lg * (n + 1.0))
            kd_ref[h] = jnp.exp(lg * (float(tb) - 1.0 - n))

    proj_ref[...] = _dot(hn_ref[0], w_ref[...])
    cos = cos_ref[...]
    sin = sin_ref[...]
    for h in range(RET_HEADS):
        q = proj_ref[:, h * RET_DK:(h + 1) * RET_DK]
        k = proj_ref[:, RET_QK + h * RET_DK:RET_QK + (h + 1) * RET_DK]
        vb = proj_ref[:, 2 * RET_QK + h * RET_DV:2 * RET_QK + (h + 1) * RET_DV].astype(BF16)
        q = q * cos + pltpu.roll(q, RET_DK // 2, 1) * sin
        k = (k * cos + pltpu.roll(k, RET_DK // 2, 1) * sin) * (RET_DK ** -0.5)
        scores = _dot_nt(q.astype(BF16), k.astype(BF16)) * dec_ref[h]
        r = r_ref[0, h]
        o = _dot(scores.astype(BF16), vb) + _dot((q * qd_ref[h]).astype(BF16), r.astype(BF16))
        r_ref[0, h] = math.exp(LOG_GAMMA[h] * tb) * r + _dot_tn((k * kd_ref[h]).astype(BF16), vb)
        o = o * lax.rsqrt(jnp.mean(o * o, axis=-1, keepdims=True) + EPS)
        g = proj_ref[:, 2 * RET_QK + RET_V + h * RET_DV:2 * RET_QK + RET_V + (h + 1) * RET_DV]
        o_ref[0, :, h * RET_DV:(h + 1) * RET_DV] = (o * _silu(g)).astype(o_ref.dtype)


def _retention(hn, w, cos2, sin2, r0, tb, chunk):
    b, t, _ = hn.shape
    nt = t // tb
    return pl.pallas_call(
        functools.partial(_ret_kernel, tb=tb, chunk=chunk),
        grid=(b, nt),
        in_specs=[
            pl.BlockSpec((1, tb, D_MODEL), lambda i, j: (i, j, 0)),
            _resident((D_MODEL, RET_COLS)),
            pl.BlockSpec((tb, RET_DK), lambda i, j: (j, 0)),
            pl.BlockSpec((tb, RET_DK), lambda i, j: (j, 0)),
            pl.BlockSpec((1, RET_HEADS, RET_DK, RET_DV), lambda i, j: (i, 0, 0, 0)),
        ],
        out_specs=[
            pl.BlockSpec((1, tb, RET_V), lambda i, j: (i, j, 0)),
            pl.BlockSpec((1, RET_HEADS, RET_DK, RET_DV), lambda i, j: (i, 0, 0, 0)),
        ],
        out_shape=[
            jax.ShapeDtypeStruct((b, t, RET_V), BF16),
            jax.ShapeDtypeStruct((b, RET_HEADS, RET_DK, RET_DV), F32),
        ],
        scratch_shapes=[
            pltpu.VMEM((tb, RET_COLS), F32),
            pltpu.VMEM((RET_HEADS, tb, tb), F32),
            pltpu.VMEM((RET_HEADS, tb, RET_DK), F32),
            pltpu.VMEM((RET_HEADS, tb, RET_DK), F32),
        ],
        compiler_params=_cparams(("parallel", "arbitrary")),
        name="retention",
    )(hn, w, cos2, sin2, r0)


def _hg_kernel(hn_ref, w_ref, lb_ref, nw_ref, s0_ref, o_ref, s_ref, proj_ref, st2_ref, kk_ref, bc_ref, qs_ref,
               acc_ref, *, chunk, n_chunks, layer, depth):
    t = pl.program_id(1)
    st_old = st2_ref.at[jnp.bitwise_and(t, 1)]
    st_new = st2_ref.at[jnp.bitwise_and(t + 1, 1)]

    @pl.when(t == 0)
    def _():
        for h in range(HG_HEADS):
            st2_ref[0, h] = s0_ref[0, h].T

    def project(lo, hi):
        proj_ref[:, lo:hi] = _dot(hn_ref[0], w_ref[:, lo:hi])

    project(HG_K, 2 * HG_K)
    project(0, HG_K)

    lrows = [lb_ref[d:d + 1, :] for d in range(depth)]
    mx = functools.reduce(jnp.maximum, lrows)
    ex = [jnp.exp(r - mx) for r in lrows]
    den = functools.reduce(lambda a, c: a + c, ex)
    lb = jnp.zeros_like(mx)
    for d in range(1, layer + 1):
        lb = lb + ex[d] / den
    one_m_lb = 1.0 - lb

    row = lax.broadcasted_iota(jnp.int32, (chunk, chunk), 0)
    col = lax.broadcasted_iota(jnp.int32, (chunk, chunk), 1)
    causal = row >= col
    tri = jnp.where(causal, 1.0, 0.0).astype(BF16)
    nw = nw_ref[...]

    def gates(c):
        sg, sgn = _sigmoid_pair(proj_ref[c * chunk:(c + 1) * chunk, HG_K:2 * HG_K])
        if layer == 0:
            fg, kk = sg, sgn
        else:
            fg, kk = lb + one_m_lb * sg, one_m_lb * sgn
        logf = jnp.log(jnp.maximum(fg, HG_MIN_F))
        g_hi = logf.astype(BF16)
        g_lo = (logf - g_hi.astype(F32)).astype(BF16)
        return kk, _dot(tri, g_hi) + _dot(tri, g_lo)

    steepest = []

    def prologue(c):
        rows = slice(c * chunk, (c + 1) * chunk)
        kk, bc = gates(c)
        total = bc[chunk - 1:chunk, :]
        steepest.append(total)
        mid = 0.5 * total
        bm = bc - mid
        qt = (_silu(proj_ref[rows, 0:HG_K]) * jnp.exp(bm)).astype(BF16)
        kt = (kk * jnp.exp(-bm)).astype(BF16)
        return dict(rows=rows, emid=jnp.exp(mid), qt=qt, kt=kt, src=st_old if c == 0 else st_new)

    def head_products(pro, h):
        sl = slice(h * HG_DK, (h + 1) * HG_DK)
        rows = pro["rows"]
        vb = proj_ref[rows, 2 * HG_K + h * HG_DV:2 * HG_K + (h + 1) * HG_DV].astype(BF16)
        qt, kt = pro["qt"][:, sl], pro["kt"][:, sl]
        return dict(vb=vb, qt=qt, emid=pro["emid"][:, sl], rows=rows, h=h, src=pro["src"],
                    scores=_dot_nt(qt, kt), kv=_dot_tn(vb, kt))

    def head_finish(hp):
        h, rows, emid = hp["h"], hp["rows"], hp["emid"]
        attn = jnp.where(causal, hp["scores"], 0.0)
        st_e = hp["src"][h] * emid
        o = _dot(attn.astype(BF16), hp["vb"]) + _dot_nt(hp["qt"], st_e.astype(BF16))
        st_new[h] = (st_e + hp["kv"]) * emid
        write_output(o, rows, h)

    def write_output(o, rows, h):
        o = o * lax.rsqrt(jnp.mean(o * o, axis=-1, keepdims=True) + EPS)
        gate = proj_ref[rows, 2 * HG_K + HG_V + h * HG_DV:2 * HG_K + HG_V + (h + 1) * HG_DV]
        o_ref[0, rows, h * HG_DV:(h + 1) * HG_DV] = (
            o * nw[:, h * HG_DV:(h + 1) * HG_DV] * _silu(gate)).astype(o_ref.dtype)

    pro = prologue(0)
    project(2 * HG_K, 2 * HG_K + HG_V)
    for c in range(n_chunks):
        products = [head_products(pro, h) for h in range(HG_HEADS)]
        if c == 0:
            project(2 * HG_K + HG_V, 2 * HG_K + 2 * HG_V)
        if c + 1 < n_chunks:
            pro = prologue(c + 1)
        for hp in products:
            head_finish(hp)

    steep = jnp.min(functools.reduce(jnp.minimum, steepest)) < -HG_SAFE_DECAY

    @pl.when(steep)
    def _():
        live_rows = lax.broadcasted_iota(jnp.int32, (chunk, HG_DV), 0)
        for c in range(n_chunks):
            rows = slice(c * chunk, (c + 1) * chunk)
            src = st_old if c == 0 else st_new
            kk, bc = gates(c)
            kk_ref[...] = kk
            bc_ref[...] = bc
            qs_ref[...] = _silu(proj_ref[rows, 0:HG_K])
            acc_ref[...] = jnp.zeros(acc_ref.shape, F32)

            def source_frame(s, carry):
                frame = pl.ds(c * chunk + s, 1)
                decay = jnp.exp(jnp.minimum(bc_ref[...] - bc_ref[pl.ds(s, 1), :], 0.0))
                term = qs_ref[...] * decay * kk_ref[pl.ds(s, 1), :]
                v_row = proj_ref[frame, 2 * HG_K:2 * HG_K + HG_V]
                for h in range(HG_HEADS):
                    sl = slice(h * HG_DK, (h + 1) * HG_DK)
                    weight = jnp.sum(term[:, sl], axis=1, keepdims=True)
                    acc_ref[:, sl] += jnp.where(live_rows >= s, weight * v_row[:, sl], 0.0)
                return carry

            lax.fori_loop(0, chunk, source_frame, 0)
            total = bc[chunk - 1:chunk, :]
            q_dec = (qs_ref[...] * jnp.exp(bc)).astype(BF16)
            k_dec = (kk * jnp.exp(total - bc)).astype(BF16)
            s_dec = jnp.exp(total)
            for h in range(HG_HEADS):
                sl = slice(h * HG_DK, (h + 1) * HG_DK)
                vb = proj_ref[rows, 2 * HG_K + h * HG_DV:2 * HG_K + (h + 1) * HG_DV].astype(BF16)
                st = src[h]
                o = acc_ref[:, sl] + _dot_nt(q_dec[:, sl], st.astype(BF16))
                st_new[h] = st * s_dec[:, sl] + _dot_tn(vb, k_dec[:, sl])
                write_output(o, rows, h)

    @pl.when(t == pl.num_programs(1) - 1)
    def _():
        for h in range(HG_HEADS):
            s_ref[0, h] = st_new[h].T


def _hgrn(hn, w, hg_lb, norm_w, s0, tb, chunk, layer):
    b, t, _ = hn.shape
    nt = t // tb
    depth = hg_lb.shape[0]
    return pl.pallas_call(
        functools.partial(_hg_kernel, chunk=chunk, n_chunks=tb // chunk, layer=layer, depth=depth),
        grid=(b, nt),
        in_specs=[
            pl.BlockSpec((1, tb, D_MODEL), lambda i, j: (i, j, 0)),
            _resident((D_MODEL, HG_COLS)),
            _resident((depth, HG_K)),
            _resident((1, HG_V)),
            pl.BlockSpec((1, HG_HEADS, HG_DK, HG_DV), lambda i, j: (i, 0, 0, 0)),
        ],
        out_specs=[
            pl.BlockSpec((1, tb, HG_V), lambda i, j: (i, j, 0)),
            pl.BlockSpec((1, HG_HEADS, HG_DK, HG_DV), lambda i, j: (i, 0, 0, 0)),
        ],
        out_shape=[
            jax.ShapeDtypeStruct((b, t, HG_V), BF16),
            jax.ShapeDtypeStruct((b, HG_HEADS, HG_DK, HG_DV), F32),
        ],
        scratch_shapes=[
            pltpu.VMEM((tb, HG_COLS), F32),
            pltpu.VMEM((2, HG_HEADS, HG_DV, HG_DK), F32),
            pltpu.VMEM((chunk, HG_K), F32),
            pltpu.VMEM((chunk, HG_K), F32),
            pltpu.VMEM((chunk, HG_K), F32),
            pltpu.VMEM((chunk, HG_V), F32),
        ],
        compiler_params=_cparams(("parallel", "arbitrary")),
        name="hgrn2",
    )(hn, w, hg_lb, norm_w.reshape(1, HG_V), s0)


def _rg_kernel(hn_ref, w_ref, cw_ref, cb_ref, wr_ref, br_ref, wi_ref, bi_ref, lam_ref, h0_ref, buf0_ref,
               o_ref, h_out_ref, buf_out_ref, proj_ref, ubuf_ref, a_ref, b_ref, perm_ref, tail_ref, cin_ref, h_ref,
               *, tb, first_pos_zero, interleaved):
    t = pl.program_id(1)
    seg = tb // SUBLANES
    step = SUBLANES if interleaved else 1
    hist = -(-(RG_CONV - 1) * step // SUBLANES) * SUBLANES
    n_hist = RG_CONV - 1

    @pl.when(t == 0)
    def _():
        tail_ref[...] = buf0_ref[0]
        h_ref[...] = h0_ref[0]
        if interleaved:
            row = lax.broadcasted_iota(jnp.int32, (tb, tb), 0)
            col = lax.broadcasted_iota(jnp.int32, (tb, tb), 1)
            frame_of = lambda p: jnp.bitwise_and(p, SUBLANES - 1) * seg + (p >> (SUBLANES.bit_length() - 1))
            perm_ref[0] = jnp.where(col == frame_of(row), 1.0, 0.0).astype(BF16)
            perm_ref[1] = jnp.where(row == frame_of(col), 1.0, 0.0).astype(BF16)

    hn = hn_ref[0]
    if interleaved:
        hn = _dot(perm_ref[0], hn).astype(BF16)
    proj_ref[...] = _dot(hn, w_ref[...])
    ubuf_ref[hist:hist + tb, :] = proj_ref[:, 0:RG_WIDTH]
    if interleaved:
        row0 = lax.broadcasted_iota(jnp.int32, (SUBLANES, RG_WIDTH), 0) == 0
        for k in range(n_hist):
            src = hist + (seg - n_hist + k) * SUBLANES
            moved = pltpu.roll(ubuf_ref[src:src + SUBLANES, :], 1, 0)
            ubuf_ref[k * SUBLANES:(k + 1) * SUBLANES, :] = jnp.where(row0, tail_ref[k:k + 1, :], moved)
        for k in range(n_hist):
            last = hist + (seg - n_hist + k) * SUBLANES + SUBLANES - 1
            tail_ref[k:k + 1, :] = ubuf_ref[last:last + 1, :]
    else:
        ubuf_ref[hist - n_hist:hist, :] = tail_ref[...]
        tail_ref[...] = ubuf_ref[hist + tb - n_hist:hist + tb, :]

    lam = lam_ref[...]
    neg_half_c_sp = (-0.5 * RG_C) * (jnp.maximum(-lam, 0.0) + jnp.log1p(jnp.exp(-jnp.abs(lam))))
    first_row = lax.broadcasted_iota(jnp.int32, (tb, RG_BLOCK), 0) == 0

    for nb in range(RG_BLOCKS):
        sl = slice(nb * RG_BLOCK, (nb + 1) * RG_BLOCK)
        xc = cb_ref[:, sl]
        for j in range(RG_CONV):
            lo = hist - (n_hist - j) * step
            xc = xc + cw_ref[j:j + 1, sl] * ubuf_ref[lo:lo + tb, sl]
        xcb = xc.astype(BF16)
        tr = _half_tanh_half(_dot(xcb, wr_ref[nb]) + br_ref[:, sl])
        ti = _half_tanh_half(_dot(xcb, wi_ref[nb]) + bi_ref[:, sl])
        log_a = neg_half_c_sp[:, sl] * (1.0 + tr)
        th = jnp.tanh(log_a)
        m2 = -2.0 * th * pl.reciprocal(1.0 - th, approx=True)
        mult = jnp.where(m2 > 0.0, m2 * lax.rsqrt(m2), 0.0)
        if first_pos_zero:
            mult = jnp.where(jnp.logical_and(first_row, t == 0), 1.0, mult)
        a_ref[:, sl] = jnp.exp(log_a)
        b_ref[:, sl] = mult * ((0.5 * xc) * (1.0 + ti))

    if interleaved:
        def tile_scan(i, c):
            hloc, prod = c
            rows = pl.ds(pl.multiple_of(i * SUBLANES, SUBLANES), SUBLANES)
            a = a_ref[rows, :]
            hloc = a * hloc + b_ref[rows, :]
            prod = a * prod
            b_ref[rows, :] = hloc
            a_ref[rows, :] = prod
            return hloc, prod

        init = (jnp.zeros((SUBLANES, RG_WIDTH), F32), jnp.ones((SUBLANES, RG_WIDTH), F32))
        hloc, prod = lax.fori_loop(0, seg, tile_scan, init, unroll=4)
        carry = h_ref[...]
        for s in range(SUBLANES):
            cin_ref[s:s + 1, :] = carry
            carry = hloc[s:s + 1, :] + prod[s:s + 1, :] * carry
        h_ref[...] = carry
        cin = cin_ref[...]

        def tile_out(i, _):
            rows = pl.ds(pl.multiple_of(i * SUBLANES, SUBLANES), SUBLANES)
            h = b_ref[rows, :] + a_ref[rows, :] * cin
            b_ref[rows, :] = h * _gelu(proj_ref[rows, RG_WIDTH:2 * RG_WIDTH])
            return 0

        lax.fori_loop(0, seg, tile_out, 0, unroll=4)
        o_ref[0] = _dot(perm_ref[1], b_ref[...].astype(BF16)).astype(o_ref.dtype)
    else:
        def step_scan(i, h):
            h = a_ref[pl.ds(i, 1), :] * h + b_ref[pl.ds(i, 1), :]
            b_ref[pl.ds(i, 1), :] = h
            return h

        h_ref[...] = lax.fori_loop(0, tb, step_scan, h_ref[...], unroll=8)
        o_ref[0] = (b_ref[...] * _gelu(proj_ref[:, RG_WIDTH:2 * RG_WIDTH])).astype(o_ref.dtype)

    @pl.when(t == pl.num_programs(1) - 1)
    def _():
        h_out_ref[0] = h_ref[...]
        buf_out_ref[0] = tail_ref[...]


def _rglru(hn, w, cw, cb, wr, br, wi, bi, lam, h0, buf0, tb, pos0):
    b, t, _ = hn.shape
    nt = t // tb
    interleaved = tb % LANES == 0
    vec = lambda a: a.reshape(1, RG_WIDTH)
    return pl.pallas_call(
        functools.partial(_rg_kernel, tb=tb, first_pos_zero=(pos0 == 0), interleaved=interleaved),
        grid=(b, nt),
        in_specs=[
            pl.BlockSpec((1, tb, D_MODEL), lambda i, j: (i, j, 0)),
            _resident((D_MODEL, RG_COLS)),
            _resident((RG_CONV, RG_WIDTH)),
            _resident((1, RG_WIDTH)),
            _resident((RG_BLOCKS, RG_BLOCK, RG_BLOCK)),
            _resident((1, RG_WIDTH)),
            _resident((RG_BLOCKS, RG_BLOCK, RG_BLOCK)),
            _resident((1, RG_WIDTH)),
            _resident((1, RG_WIDTH)),
            pl.BlockSpec((1, 1, RG_WIDTH), lambda i, j: (i, 0, 0)),
            pl.BlockSpec((1, RG_CONV - 1, RG_WIDTH), lambda i, j: (i, 0, 0)),
        ],
        out_specs=[
            pl.BlockSpec((1, tb, RG_WIDTH), lambda i, j: (i, j, 0)),
            pl.BlockSpec((1, 1, RG_WIDTH), lambda i, j: (i, 0, 0)),
            pl.BlockSpec((1, RG_CONV - 1, RG_WIDTH), lambda i, j: (i, 0, 0)),
        ],
        out_shape=[
            jax.ShapeDtypeStruct((b, t, RG_WIDTH), BF16),
            jax.ShapeDtypeStruct((b, 1, RG_WIDTH), F32),
            jax.ShapeDtypeStruct((b, RG_CONV - 1, RG_WIDTH), F32),
        ],
        scratch_shapes=[
            pltpu.VMEM((tb, RG_COLS), F32),
            pltpu.VMEM((tb + (RG_CONV - 1) * SUBLANES, RG_WIDTH), F32),
            pltpu.VMEM((tb, RG_WIDTH), F32),
            pltpu.VMEM((tb, RG_WIDTH), F32),
            pltpu.VMEM((2, tb, tb), BF16),
            pltpu.VMEM((RG_CONV - 1, RG_WIDTH), F32),
            pltpu.VMEM((SUBLANES, RG_WIDTH), F32),
            pltpu.VMEM((1, RG_WIDTH), F32),
        ],
        compiler_params=_cparams(("parallel", "arbitrary")),
        name="rglru",
    )(hn, w, cw, vec(cb), wr, vec(br), wi, vec(bi), vec(lam), h0.reshape(b, 1, RG_WIDTH), buf0)


def _merge_kernel(hn_ref, oret_ref, ohg_ref, org_ref, x_ref, wg_ref, wb_ref, wo_ref, out_ref):
    hn = hn_ref[...]
    mixed = _sigmoid(_dot(hn, wg_ref[:, 0:D_MODEL])) * _dot(oret_ref[...], wb_ref[0:RET_V, :])
    mixed = mixed + _sigmoid(_dot(hn, wg_ref[:, D_MODEL:2 * D_MODEL])) * _dot(
        ohg_ref[...], wb_ref[RET_V:RET_V + HG_V, :])
    mixed = mixed + _sigmoid(_dot(hn, wg_ref[:, 2 * D_MODEL:3 * D_MODEL])) * _dot(
        org_ref[...], wb_ref[RET_V + HG_V:RET_V + HG_V + RG_WIDTH, :])
    out_ref[...] = x_ref[...] + _dot(mixed.astype(BF16), wo_ref[...])


def _merge(hn, o_ret, o_hg, o_rg, x, wg, wb, wo, tm):
    rows = x.shape[0]
    rowspec = lambda width: pl.BlockSpec((tm, width), lambda i: (i, 0))
    return pl.pallas_call(
        _merge_kernel,
        grid=(rows // tm,),
        in_specs=[rowspec(D_MODEL), rowspec(RET_V), rowspec(HG_V), rowspec(RG_WIDTH), rowspec(D_MODEL),
                  _resident((D_MODEL, GATE_COLS)), _resident((RET_V + HG_V + RG_WIDTH, D_MODEL)),
                  _resident((D_MODEL, D_MODEL))],
        out_specs=rowspec(D_MODEL),
        out_shape=jax.ShapeDtypeStruct((rows, D_MODEL), F32),
        compiler_params=_cparams(("parallel",)),
        name="merge",
    )(hn, o_ret, o_hg, o_rg, x, wg, wb, wo)


def _ffn_kernel(x_ref, n2_ref, wup_ref, cw_ref, cb_ref, wdn_ref, buf0_ref, fn_ref, out_ref, buf_out_ref,
                abuf_ref, *, tb, final_norm):
    t = pl.program_id(1)
    hist = SUBLANES

    @pl.when(t == 0)
    def _():
        abuf_ref[0:hist, :] = jnp.zeros((hist, D_FF), F32)
        abuf_ref[hist - (FFN_CONV - 1):hist, :] = buf0_ref[0]

    x = x_ref[0]
    hn = _rms(x, n2_ref[...]).astype(BF16)
    abuf_ref[hist:hist + tb, :] = _dot(hn, wup_ref[:, 0:D_FF])
    gate = _dot(hn, wup_ref[:, D_FF:2 * D_FF])
    a = cb_ref[...]
    for j in range(FFN_CONV):
        a = a + cw_ref[j:j + 1, :] * abuf_ref[hist - (FFN_CONV - 1) + j:hist - (FFN_CONV - 1) + j + tb, :]
    y = x + _dot((_gelu(a) * gate).astype(BF16), wdn_ref[...])
    if final_norm:
        y = _rms(y, fn_ref[...])
    out_ref[0] = y

    @pl.when(t == pl.num_programs(1) - 1)
    def _():
        buf_out_ref[0] = abuf_ref[tb + hist - (FFN_CONV - 1):tb + hist, :]

    abuf_ref[0:hist, :] = abuf_ref[tb:tb + hist, :]


def _convffn(x, n2, wup, cw, cb, wdn, buf0, fn, tb, final_norm):
    b, t, _ = x.shape
    nt = t // tb
    return pl.pallas_call(
        functools.partial(_ffn_kernel, tb=tb, final_norm=final_norm),
        grid=(b, nt),
        in_specs=[
            pl.BlockSpec((1, tb, D_MODEL), lambda i, j: (i, j, 0)),
            _resident((1, D_MODEL)),
            _resident((D_MODEL, 2 * D_FF)),
            _resident((FFN_CONV, D_FF)),
            _resident((1, D_FF)),
            _resident((D_FF, D_MODEL)),
            pl.BlockSpec((1, FFN_CONV - 1, D_FF), lambda i, j: (i, 0, 0)),
            _resident((1, D_MODEL)),
        ],
        out_specs=[
            pl.BlockSpec((1, tb, D_MODEL), lambda i, j: (i, j, 0)),
            pl.BlockSpec((1, FFN_CONV - 1, D_FF), lambda i, j: (i, 0, 0)),
        ],
        out_shape=[
            jax.ShapeDtypeStruct((b, t, D_MODEL), F32),
            jax.ShapeDtypeStruct((b, FFN_CONV - 1, D_FF), F32),
        ],
        scratch_shapes=[pltpu.VMEM((tb + SUBLANES, D_FF), F32)],
        compiler_params=_cparams(("parallel", "arbitrary")),
        name="convffn",
    )(x, n2.reshape(1, D_MODEL), wup, cw, cb.reshape(1, D_FF), wdn, buf0, fn.reshape(1, D_MODEL))


def _rope_tables(t, pos0):
    half = RET_DK // 2
    inv = jnp.power(ROPE_BASE, -jnp.arange(half, dtype=F32) / half)
    ang = (jnp.arange(t, dtype=F32) + pos0)[:, None] * inv[None, :]
    cos, sin = jnp.cos(ang), jnp.sin(ang)
    return jnp.concatenate([cos, cos], axis=-1), jnp.concatenate([-sin, sin], axis=-1)


def _block_sizes(t):
    if t <= CHUNK:
        return t, t
    return TIME_BLOCK, CHUNK


def _trunk(x, pos0, r0, s0, h0, rgb0, ffb0, p):
    b, t, _ = x.shape
    tb, chunk = _block_sizes(t)
    rows = b * t
    tm = min(rows, ROW_BLOCK)
    cos2, sin2 = _rope_tables(t, pos0)
    depth = p["norm1_w"].shape[0]
    states = []
    x2d = x.reshape(rows, D_MODEL)
    for l in range(depth):
        hn2d = _norm(x2d, p["norm1_w"][l], tm)
        hn = hn2d.reshape(b, t, D_MODEL)
        o_ret, r_new = _retention(hn, p["w_ret"][l], cos2, sin2, r0[l], tb, chunk)
        o_hg, s_new = _hgrn(hn, p["w_hg"][l], p["hg_lb"], p["hg_norm_w"][l], s0[l], tb, chunk, l)
        o_rg, h_new, rgb_new = _rglru(
            hn, p["w_rg"][l], p["rg_conv_w"][l], p["rg_conv_b"][l], p["rg_w_r"][l], p["rg_b_r"][l],
            p["rg_w_i"][l], p["rg_b_i"][l], p["rg_lambda"][l], h0[l], rgb0[l], tb, pos0)
        x2d = _merge(hn2d, o_ret.reshape(rows, RET_V), o_hg.reshape(rows, HG_V), o_rg.reshape(rows, RG_WIDTH),
                     x2d, p["w_gate"][l], p["w_branch"][l], p["w_out"][l], tm)
        x3d, ffb_new = _convffn(
            x2d.reshape(b, t, D_MODEL), p["norm2_w"][l], p["w_up"][l], p["ffn_conv_w"][l], p["ffn_conv_b"][l],
            p["w_down"][l], ffb0[l], p["final_norm_w"], tb, l == depth - 1)
        x2d = x3d.reshape(rows, D_MODEL)
        states.append((r_new, s_new, h_new.reshape(b, RG_WIDTH), rgb_new, ffb_new))
    new_states = tuple(jnp.stack(st, axis=0) for st in zip(*states))
    return x2d.reshape(b, t, D_MODEL), new_states


def kernel(x_prompt, x_sample, state_ret, state_hgrn, state_rglru, cache_rg_conv, cache_ffn_conv,
           norm1_w, w_in, w_branch, w_out, rg_conv_w, rg_conv_b, rg_w_r, rg_b_r, rg_w_i, rg_b_i,
           rg_lambda, hg_lb, hg_norm_w, norm2_w, w_up, ffn_conv_w, ffn_conv_b, w_down, final_norm_w):
    depth = w_in.shape[0]
    batch = x_prompt.shape[0]
    bf = lambda a: a.astype(BF16)
    p = dict(
        norm1_w=norm1_w, norm2_w=norm2_w, final_norm_w=final_norm_w, hg_lb=hg_lb, hg_norm_w=hg_norm_w,
        w_ret=bf(w_in[:, :, 0:OFF_HG]), w_hg=bf(w_in[:, :, OFF_HG:OFF_RG]),
        w_rg=bf(w_in[:, :, OFF_RG:OFF_GATE]), w_gate=bf(w_in[:, :, OFF_GATE:OFF_GATE + GATE_COLS]),
        w_branch=bf(w_branch), w_out=bf(w_out), rg_conv_w=rg_conv_w, rg_conv_b=rg_conv_b,
        rg_w_r=bf(rg_w_r), rg_b_r=rg_b_r, rg_w_i=bf(rg_w_i), rg_b_i=rg_b_i, rg_lambda=rg_lambda,
        w_up=bf(w_up), ffn_conv_w=ffn_conv_w, ffn_conv_b=ffn_conv_b, w_down=bf(w_down),
    )
    zeros = lambda *shape: jnp.zeros((depth, batch) + shape, F32)
    y_p, (ret_p, hg_p, rgh_p, rgc_p, ffc_p) = _trunk(
        x_prompt, 0, zeros(RET_HEADS, RET_DK, RET_DV), zeros(HG_HEADS, HG_DK, HG_DV), zeros(RG_WIDTH),
        zeros(RG_CONV - 1, RG_WIDTH), zeros(FFN_CONV - 1, D_FF), p)
    y_s, (ret_s, hg_s, rgh_s, rgc_s, ffc_s) = _trunk(
        x_sample, PAST_LEN, state_ret, state_hgrn, state_rglru, cache_rg_conv, cache_ffn_conv, p)
    return (y_p, y_s, ret_p, ret_s, hg_p, hg_s, rgh_p, rgh_s, rgc_p, rgc_s, ffc_p, ffc_s)
```

```python
import functools
import math

import jax
import jax.numpy as jnp
from jax import lax
from jax.experimental import pallas as pl
from jax.experimental.pallas import tpu as pltpu

F32 = jnp.float32
BF16 = jnp.bfloat16

D_MODEL = 1024
PAST_LEN = 2048
CHUNK = 64
RET_HEADS, RET_DK, RET_DV = 4, 128, 256
RET_QK = RET_HEADS * RET_DK
RET_V = RET_HEADS * RET_DV
ROPE_BASE = 10000.0
HG_HEADS, HG_DK, HG_DV = 8, 128, 128
HG_K = HG_HEADS * HG_DK
HG_V = HG_HEADS * HG_DV
HG_MIN_F = 1e-6
HG_SAFE_DECAY = 150.0
RG_BLOCKS, RG_BLOCK = 5, 256
RG_WIDTH = RG_BLOCKS * RG_BLOCK
RG_CONV = 4
RG_C = 8.0
D_FF = 2816
FFN_CONV = 3
EPS = 1e-6

RET_COLS = 2 * RET_QK + 2 * RET_V
HG_COLS = 2 * HG_K + 2 * HG_V
RG_COLS = 2 * RG_WIDTH
GATE_COLS = 3 * D_MODEL
OFF_HG = RET_COLS
OFF_RG = OFF_HG + HG_COLS
OFF_GATE = OFF_RG + RG_COLS

VMEM_LIMIT_BYTES = 56 * 1024 * 1024
SUBLANES = 8
LANES = 128
TIME_BLOCK = 256
ROW_BLOCK = 512
GELU_C = math.sqrt(2.0 / math.pi)

LOG_GAMMA = tuple(math.log1p(-(2.0 ** (-5.0 - h))) for h in range(RET_HEADS))


def _cparams(sem):
    return pltpu.CompilerParams(dimension_semantics=sem, vmem_limit_bytes=VMEM_LIMIT_BYTES)


def _resident(shape):
    nd = len(shape)
    return pl.BlockSpec(shape, lambda *_: (0,) * nd, pipeline_mode=pl.Buffered(1))


def _sigmoid_pair(z):
    e = jnp.exp(-jnp.abs(z))
    d = pl.reciprocal(1.0 + e, approx=True)
    ed = e * d
    pos = z >= 0.0
    return jnp.where(pos, d, ed), jnp.where(pos, ed, d)


def _half_tanh_half(x):
    return jnp.tanh(0.5 * x)


def _sigmoid(x):
    return 0.5 * (1.0 + _half_tanh_half(x))


def _silu(x):
    return (0.5 * x) * (1.0 + _half_tanh_half(x))


def _gelu(x):
    inner = x * (GELU_C + (GELU_C * 0.044715) * (x * x))
    return (0.5 * x) * (1.0 + jnp.tanh(inner))


def _rms(x, w):
    ms = jnp.mean(x * x, axis=-1, keepdims=True)
    return x * lax.rsqrt(ms + EPS) * w


def _dot(a, b):
    return jnp.dot(a, b, preferred_element_type=F32)


def _dot_nt(a, b):
    return lax.dot_general(a, b, (((1,), (1,)), ((), ())), preferred_element_type=F32)


def _dot_tn(a, b):
    return lax.dot_general(a, b, (((0,), (0,)), ((), ())), preferred_element_type=F32)


def _ret_kernel(x_ref, n1_ref, w_ref, cos_ref, sin_ref, r0_ref, o_ref, r_ref, hn_ref, proj_ref, dec_ref, qd_ref,
                kd_ref, *, tb, chunk):
    t = pl.program_id(1)

    @pl.when(t == 0)
    def _():
        r_ref[...] = r0_ref[...]
        row = lax.broadcasted_iota(jnp.int32, (tb, tb), 0)
        col = lax.broadcasted_iota(jnp.int32, (tb, tb), 1)
        dist = jnp.abs(row - col).astype(F32)
        chunk_bits = chunk.bit_length() - 1
        later_chunk = col >= ((row >> chunk_bits) + 1) * chunk
        n = lax.broadcasted_iota(jnp.int32, (tb, RET_DK), 0).astype(F32)
        for h in range(RET_HEADS):
            lg = LOG_GAMMA[h]
            dec_ref[h] = jnp.where(later_chunk, 0.0, jnp.exp(dist * lg))
            qd_ref[h] = jnp.exp(lg * (n + 1.0))
            kd_ref[h] = jnp.exp(lg * (float(tb) - 1.0 - n))

    hn = _rms(x_ref[0], n1_ref[...]).astype(BF16)
    hn_ref[0] = hn
    proj_ref[...] = _dot(hn, w_ref[...])
    cos = cos_ref[...]
    sin = sin_ref[...]
    for h in range(RET_HEADS):
        q = proj_ref[:, h * RET_DK:(h + 1) * RET_DK]
        k = proj_ref[:, RET_QK + h * RET_DK:RET_QK + (h + 1) * RET_DK]
        vb = proj_ref[:, 2 * RET_QK + h * RET_DV:2 * RET_QK + (h + 1) * RET_DV].astype(BF16)
        q = q * cos + pltpu.roll(q, RET_DK // 2, 1) * sin
        k = (k * cos + pltpu.roll(k, RET_DK // 2, 1) * sin) * (RET_DK ** -0.5)
        scores = _dot_nt(q.astype(BF16), k.astype(BF16)) * dec_ref[h]
        r = r_ref[0, h]
        o = _dot(scores.astype(BF16), vb) + _dot((q * qd_ref[h]).astype(BF16), r.astype(BF16))
        r_ref[0, h] = math.exp(LOG_GAMMA[h] * tb) * r + _dot_tn((k * kd_ref[h]).astype(BF16), vb)
        o = o * lax.rsqrt(jnp.mean(o * o, axis=-1, keepdims=True) + EPS)
        g = proj_ref[:, 2 * RET_QK + RET_V + h * RET_DV:2 * RET_QK + RET_V + (h + 1) * RET_DV]
        o_ref[0, :, h * RET_DV:(h + 1) * RET_DV] = (o * _silu(g)).astype(o_ref.dtype)


def _retention(x, n1, w, cos2, sin2, r0, tb, chunk):
    b, t, _ = x.shape
    nt = t // tb
    return pl.pallas_call(
        functools.partial(_ret_kernel, tb=tb, chunk=chunk),
        grid=(b, nt),
        in_specs=[
            pl.BlockSpec((1, tb, D_MODEL), lambda i, j: (i, j, 0)),
            _resident((1, D_MODEL)),
            _resident((D_MODEL, RET_COLS)),
            pl.BlockSpec((tb, RET_DK), lambda i, j: (j, 0)),
            pl.BlockSpec((tb, RET_DK), lambda i, j: (j, 0)),
            pl.BlockSpec((1, RET_HEADS, RET_DK, RET_DV), lambda i, j: (i, 0, 0, 0)),
        ],
        out_specs=[
            pl.BlockSpec((1, tb, RET_V), lambda i, j: (i, j, 0)),
            pl.BlockSpec((1, RET_HEADS, RET_DK, RET_DV), lambda i, j: (i, 0, 0, 0)),
            pl.BlockSpec((1, tb, D_MODEL), lambda i, j: (i, j, 0)),
        ],
        out_shape=[
            jax.ShapeDtypeStruct((b, t, RET_V), BF16),
            jax.ShapeDtypeStruct((b, RET_HEADS, RET_DK, RET_DV), F32),
            jax.ShapeDtypeStruct((b, t, D_MODEL), BF16),
        ],
        scratch_shapes=[
            pltpu.VMEM((tb, RET_COLS), F32),
            pltpu.VMEM((RET_HEADS, tb, tb), F32),
            pltpu.VMEM((RET_HEADS, tb, RET_DK), F32),
            pltpu.VMEM((RET_HEADS, tb, RET_DK), F32),
        ],
        compiler_params=_cparams(("parallel", "arbitrary")),
        name="retention",
    )(x, n1.reshape(1, D_MODEL), w, cos2, sin2, r0)


def _hg_kernel(hn_ref, w_ref, lb_ref, nw_ref, s0_ref, o_ref, s_ref, proj_ref, st2_ref, kk_ref, bc_ref, qs_ref,
               acc_ref, *, chunk, n_chunks, layer, depth):
    t = pl.program_id(1)
    st_old = st2_ref.at[jnp.bitwise_and(t, 1)]
    st_new = st2_ref.at[jnp.bitwise_and(t + 1, 1)]

    @pl.when(t == 0)
    def _():
        for h in range(HG_HEADS):
            st2_ref[0, h] = s0_ref[0, h].T

    def project(lo, hi):
        proj_ref[:, lo:hi] = _dot(hn_ref[0], w_ref[:, lo:hi])

    project(HG_K, 2 * HG_K)
    project(0, HG_K)

    lrows = [lb_ref[d:d + 1, :] for d in range(depth)]
    mx = functools.reduce(jnp.maximum, lrows)
    ex = [jnp.exp(r - mx) for r in lrows]
    den = functools.reduce(lambda a, c: a + c, ex)
    lb = jnp.zeros_like(mx)
    for d in range(1, layer + 1):
        lb = lb + ex[d] / den
    one_m_lb = 1.0 - lb

    row = lax.broadcasted_iota(jnp.int32, (chunk, chunk), 0)
    col = lax.broadcasted_iota(jnp.int32, (chunk, chunk), 1)
    causal = row >= col
    tri = jnp.where(causal, 1.0, 0.0).astype(BF16)
    nw = nw_ref[...]

    def gates(c):
        sg, sgn = _sigmoid_pair(proj_ref[c * chunk:(c + 1) * chunk, HG_K:2 * HG_K])
        if layer == 0:
            fg, kk = sg, sgn
        else:
            fg, kk = lb + one_m_lb * sg, one_m_lb * sgn
        logf = jnp.log(jnp.maximum(fg, HG_MIN_F))
        g_hi = logf.astype(BF16)
        g_lo = (logf - g_hi.astype(F32)).astype(BF16)
        return kk, _dot(tri, g_hi) + _dot(tri, g_lo)

    steepest = []

    def prologue(c):
        rows = slice(c * chunk, (c + 1) * chunk)
        kk, bc = gates(c)
        total = bc[chunk - 1:chunk, :]
        steepest.append(total)
        mid = 0.5 * total
        bm = bc - mid
        qt = (_silu(proj_ref[rows, 0:HG_K]) * jnp.exp(bm)).astype(BF16)
        kt = (kk * jnp.exp(-bm)).astype(BF16)
        return dict(rows=rows, emid=jnp.exp(mid), qt=qt, kt=kt, src=st_old if c == 0 else st_new)

    def head_products(pro, h):
        sl = slice(h * HG_DK, (h + 1) * HG_DK)
        rows = pro["rows"]
        vb = proj_ref[rows, 2 * HG_K + h * HG_DV:2 * HG_K + (h + 1) * HG_DV].astype(BF16)
        qt, kt = pro["qt"][:, sl], pro["kt"][:, sl]
        return dict(vb=vb, qt=qt, emid=pro["emid"][:, sl], rows=rows, h=h, src=pro["src"],
                    scores=_dot_nt(qt, kt), kv=_dot_tn(vb, kt))

    def head_finish(hp):
        h, rows, emid = hp["h"], hp["rows"], hp["emid"]
        attn = jnp.where(causal, hp["scores"], 0.0)
        st_e = hp["src"][h] * emid
        o = _dot(attn.astype(BF16), hp["vb"]) + _dot_nt(hp["qt"], st_e.astype(BF16))
        st_new[h] = (st_e + hp["kv"]) * emid
        write_output(o, rows, h)

    def write_output(o, rows, h):
        o = o * lax.rsqrt(jnp.mean(o * o, axis=-1, keepdims=True) + EPS)
        gate = proj_ref[rows, 2 * HG_K + HG_V + h * HG_DV:2 * HG_K + HG_V + (h + 1) * HG_DV]
        o_ref[0, rows, h * HG_DV:(h + 1) * HG_DV] = (
            o * nw[:, h * HG_DV:(h + 1) * HG_DV] * _silu(gate)).astype(o_ref.dtype)

    pro = prologue(0)
    project(2 * HG_K, 2 * HG_K + HG_V)
    for c in range(n_chunks):
        products = [head_products(pro, h) for h in range(HG_HEADS)]
        if c == 0:
            project(2 * HG_K + HG_V, 2 * HG_K + 2 * HG_V)
        if c + 1 < n_chunks:
            pro = prologue(c + 1)
        for hp in products:
            head_finish(hp)

    steep = jnp.min(functools.reduce(jnp.minimum, steepest)) < -HG_SAFE_DECAY

    @pl.when(steep)
    def _():
        live_rows = lax.broadcasted_iota(jnp.int32, (chunk, HG_DV), 0)
        for c in range(n_chunks):
            rows = slice(c * chunk, (c + 1) * chunk)
            src = st_old if c == 0 else st_new
            kk, bc = gates(c)
            kk_ref[...] = kk
            bc_ref[...] = bc
            qs_ref[...] = _silu(proj_ref[rows, 0:HG_K])
            acc_ref[...] = jnp.zeros(acc_ref.shape, F32)

            def source_frame(s, carry):
                frame = pl.ds(c * chunk + s, 1)
                decay = jnp.exp(jnp.minimum(bc_ref[...] - bc_ref[pl.ds(s, 1), :], 0.0))
                term = qs_ref[...] * decay * kk_ref[pl.ds(s, 1), :]
                v_row = proj_ref[frame, 2 * HG_K:2 * HG_K + HG_V]
                for h in range(HG_HEADS):
                    sl = slice(h * HG_DK, (h + 1) * HG_DK)
                    weight = jnp.sum(term[:, sl], axis=1, keepdims=True)
                    acc_ref[:, sl] += jnp.where(live_rows >= s, weight * v_row[:, sl], 0.0)
                return carry

            lax.fori_loop(0, chunk, source_frame, 0)
            total = bc[chunk - 1:chunk, :]
            q_dec = (qs_ref[...] * jnp.exp(bc)).astype(BF16)
            k_dec = (kk * jnp.exp(total - bc)).astype(BF16)
            s_dec = jnp.exp(total)
            for h in range(HG_HEADS):
                sl = slice(h * HG_DK, (h + 1) * HG_DK)
                vb = proj_ref[rows, 2 * HG_K + h * HG_DV:2 * HG_K + (h + 1) * HG_DV].astype(BF16)
                st = src[h]
                o = acc_ref[:, sl] + _dot_nt(q_dec[:, sl], st.astype(BF16))
                st_new[h] = st * s_dec[:, sl] + _dot_tn(vb, k_dec[:, sl])
                write_output(o, rows, h)

    @pl.when(t == pl.num_programs(1) - 1)
    def _():
        for h in range(HG_HEADS):
            s_ref[0, h] = st_new[h].T


def _hgrn(hn, w, hg_lb, norm_w, s0, tb, chunk, layer):
    b, t, _ = hn.shape
    nt = t // tb
    depth = hg_lb.shape[0]
    return pl.pallas_call(
        functools.partial(_hg_kernel, chunk=chunk, n_chunks=tb // chunk, layer=layer, depth=depth),
        grid=(b, nt),
        in_specs=[
            pl.BlockSpec((1, tb, D_MODEL), lambda i, j: (i, j, 0)),
            _resident((D_MODEL, HG_COLS)),
            _resident((depth, HG_K)),
            _resident((1, HG_V)),
            pl.BlockSpec((1, HG_HEADS, HG_DK, HG_DV), lambda i, j: (i, 0, 0, 0)),
        ],
        out_specs=[
            pl.BlockSpec((1, tb, HG_V), lambda i, j: (i, j, 0)),
            pl.BlockSpec((1, HG_HEADS, HG_DK, HG_DV), lambda i, j: (i, 0, 0, 0)),
        ],
        out_shape=[
            jax.ShapeDtypeStruct((b, t, HG_V), BF16),
            jax.ShapeDtypeStruct((b, HG_HEADS, HG_DK, HG_DV), F32),
        ],
        scratch_shapes=[
            pltpu.VMEM((tb, HG_COLS), F32),
            pltpu.VMEM((2, HG_HEADS, HG_DV, HG_DK), F32),
            pltpu.VMEM((chunk, HG_K), F32),
            pltpu.VMEM((chunk, HG_K), F32),
            pltpu.VMEM((chunk, HG_K), F32),
            pltpu.VMEM((chunk, HG_V), F32),
        ],
        compiler_params=_cparams(("parallel", "arbitrary")),
        name="hgrn2",
    )(hn, w, hg_lb, norm_w.reshape(1, HG_V), s0)


def _rg_kernel(hn_ref, w_ref, cw_ref, cb_ref, wr_ref, br_ref, wi_ref, bi_ref, lam_ref, h0_ref, buf0_ref,
               o_ref, h_out_ref, buf_out_ref, gate_ref, ubuf_ref, a_ref, b_ref, perm_ref, tail_ref, cin_ref, h_ref,
               *, tb, first_pos_zero, interleaved):
    t = pl.program_id(1)
    seg = tb // SUBLANES
    step = SUBLANES if interleaved else 1
    hist = -(-(RG_CONV - 1) * step // SUBLANES) * SUBLANES
    n_hist = RG_CONV - 1

    @pl.when(t == 0)
    def _():
        tail_ref[...] = buf0_ref[0]
        h_ref[...] = h0_ref[0]
        if interleaved:
            row = lax.broadcasted_iota(jnp.int32, (tb, tb), 0)
            col = lax.broadcasted_iota(jnp.int32, (tb, tb), 1)
            frame_of = lambda p: jnp.bitwise_and(p, SUBLANES - 1) * seg + (p >> (SUBLANES.bit_length() - 1))
            perm_ref[0] = jnp.where(col == frame_of(row), 1.0, 0.0).astype(BF16)
            perm_ref[1] = jnp.where(row == frame_of(col), 1.0, 0.0).astype(BF16)

    hn = hn_ref[0]
    if interleaved:
        hn = _dot(perm_ref[0], hn).astype(BF16)
    gate_ref[...] = _dot(hn, w_ref[:, RG_WIDTH:2 * RG_WIDTH])
    gate_ref[...] = _gelu(gate_ref[...])

    ubuf_ref[hist:hist + tb, :] = _dot(hn, w_ref[:, 0:RG_WIDTH])
    if interleaved:
        row0 = lax.broadcasted_iota(jnp.int32, (SUBLANES, RG_WIDTH), 0) == 0
        for k in range(n_hist):
            src = hist + (seg - n_hist + k) * SUBLANES
            moved = pltpu.roll(ubuf_ref[src:src + SUBLANES, :], 1, 0)
            ubuf_ref[k * SUBLANES:(k + 1) * SUBLANES, :] = jnp.where(row0, tail_ref[k:k + 1, :], moved)
        for k in range(n_hist):
            last = hist + (seg - n_hist + k) * SUBLANES + SUBLANES - 1
            tail_ref[k:k + 1, :] = ubuf_ref[last:last + 1, :]
    else:
        ubuf_ref[hist - n_hist:hist, :] = tail_ref[...]
        tail_ref[...] = ubuf_ref[hist + tb - n_hist:hist + tb, :]

    lam = lam_ref[...]
    neg_half_c_sp = (-0.5 * RG_C) * (jnp.maximum(-lam, 0.0) + jnp.log1p(jnp.exp(-jnp.abs(lam))))
    first_row = lax.broadcasted_iota(jnp.int32, (tb, RG_BLOCK), 0) == 0

    for nb in range(RG_BLOCKS):
        sl = slice(nb * RG_BLOCK, (nb + 1) * RG_BLOCK)
        xc = cb_ref[:, sl]
        for j in range(RG_CONV):
            lo = hist - (n_hist - j) * step
            xc = xc + cw_ref[j:j + 1, sl] * ubuf_ref[lo:lo + tb, sl]
        xcb = xc.astype(BF16)
        tr = _half_tanh_half(_dot(xcb, wr_ref[nb]) + br_ref[:, sl])
        ti = _half_tanh_half(_dot(xcb, wi_ref[nb]) + bi_ref[:, sl])
        log_a = neg_half_c_sp[:, sl] * (1.0 + tr)
        th = jnp.tanh(log_a)
        m2 = -2.0 * th * pl.reciprocal(1.0 - th, approx=True)
        mult = jnp.where(m2 > 0.0, m2 * lax.rsqrt(m2), 0.0)
        if first_pos_zero:
            mult = jnp.where(jnp.logical_and(first_row, t == 0), 1.0, mult)
        a_ref[:, sl] = jnp.exp(log_a)
        b_ref[:, sl] = mult * ((0.5 * xc) * (1.0 + ti))

    if interleaved:
        def tile_scan(i, c):
            hloc, prod = c
            rows = pl.ds(pl.multiple_of(i * SUBLANES, SUBLANES), SUBLANES)
            a = a_ref[rows, :]
            hloc = a * hloc + b_ref[rows, :]
            prod = a * prod
            b_ref[rows, :] = hloc
            a_ref[rows, :] = prod
            return hloc, prod

        init = (jnp.zeros((SUBLANES, RG_WIDTH), F32), jnp.ones((SUBLANES, RG_WIDTH), F32))
        hloc, prod = lax.fori_loop(0, seg, tile_scan, init, unroll=4)
        carry = h_ref[...]
        for s in range(SUBLANES):
            cin_ref[s:s + 1, :] = carry
            carry = hloc[s:s + 1, :] + prod[s:s + 1, :] * carry
        h_ref[...] = carry
        cin = cin_ref[...]

        def tile_out(i, _):
            rows = pl.ds(pl.multiple_of(i * SUBLANES, SUBLANES), SUBLANES)
            h = b_ref[rows, :] + a_ref[rows, :] * cin
            b_ref[rows, :] = h * gate_ref[rows, :]
            return 0

        lax.fori_loop(0, seg, tile_out, 0, unroll=4)
        o_ref[0] = _dot(perm_ref[1], b_ref[...].astype(BF16)).astype(o_ref.dtype)
    else:
        def step_scan(i, h):
            h = a_ref[pl.ds(i, 1), :] * h + b_ref[pl.ds(i, 1), :]
            b_ref[pl.ds(i, 1), :] = h
            return h

        h_ref[...] = lax.fori_loop(0, tb, step_scan, h_ref[...], unroll=8)
        o_ref[0] = (b_ref[...] * gate_ref[...]).astype(o_ref.dtype)

    @pl.when(t == pl.num_programs(1) - 1)
    def _():
        h_out_ref[0] = h_ref[...]
        buf_out_ref[0] = tail_ref[...]


def _rglru(hn, w, cw, cb, wr, br, wi, bi, lam, h0, buf0, tb, pos0):
    b, t, _ = hn.shape
    nt = t // tb
    interleaved = tb % LANES == 0
    vec = lambda a: a.reshape(1, RG_WIDTH)
    return pl.pallas_call(
        functools.partial(_rg_kernel, tb=tb, first_pos_zero=(pos0 == 0), interleaved=interleaved),
        grid=(b, nt),
        in_specs=[
            pl.BlockSpec((1, tb, D_MODEL), lambda i, j: (i, j, 0)),
            _resident((D_MODEL, RG_COLS)),
            _resident((RG_CONV, RG_WIDTH)),
            _resident((1, RG_WIDTH)),
            _resident((RG_BLOCKS, RG_BLOCK, RG_BLOCK)),
            _resident((1, RG_WIDTH)),
            _resident((RG_BLOCKS, RG_BLOCK, RG_BLOCK)),
            _resident((1, RG_WIDTH)),
            _resident((1, RG_WIDTH)),
            pl.BlockSpec((1, 1, RG_WIDTH), lambda i, j: (i, 0, 0)),
            pl.BlockSpec((1, RG_CONV - 1, RG_WIDTH), lambda i, j: (i, 0, 0)),
        ],
        out_specs=[
            pl.BlockSpec((1, tb, RG_WIDTH), lambda i, j: (i, j, 0)),
            pl.BlockSpec((1, 1, RG_WIDTH), lambda i, j: (i, 0, 0)),
            pl.BlockSpec((1, RG_CONV - 1, RG_WIDTH), lambda i, j: (i, 0, 0)),
        ],
        out_shape=[
            jax.ShapeDtypeStruct((b, t, RG_WIDTH), BF16),
            jax.ShapeDtypeStruct((b, 1, RG_WIDTH), F32),
            jax.ShapeDtypeStruct((b, RG_CONV - 1, RG_WIDTH), F32),
        ],
        scratch_shapes=[
            pltpu.VMEM((tb, RG_WIDTH), F32),
            pltpu.VMEM((tb + (RG_CONV - 1) * SUBLANES, RG_WIDTH), F32),
            pltpu.VMEM((tb, RG_WIDTH), F32),
            pltpu.VMEM((tb, RG_WIDTH), F32),
            pltpu.VMEM((2, tb, tb), BF16),
            pltpu.VMEM((RG_CONV - 1, RG_WIDTH), F32),
            pltpu.VMEM((SUBLANES, RG_WIDTH), F32),
            pltpu.VMEM((1, RG_WIDTH), F32),
        ],
        compiler_params=_cparams(("parallel", "arbitrary")),
        name="rglru",
    )(hn, w, cw, vec(cb), wr, vec(br), wi, vec(bi), vec(lam), h0.reshape(b, 1, RG_WIDTH), buf0)


def _merge_kernel(hn_ref, oret_ref, ohg_ref, org_ref, x_ref, wg_ref, wb_ref, wo_ref, out_ref):
    hn = hn_ref[...]
    mixed = _sigmoid(_dot(hn, wg_ref[:, 0:D_MODEL])) * _dot(oret_ref[...], wb_ref[0:RET_V, :])
    mixed = mixed + _sigmoid(_dot(hn, wg_ref[:, D_MODEL:2 * D_MODEL])) * _dot(
        ohg_ref[...], wb_ref[RET_V:RET_V + HG_V, :])
    mixed = mixed + _sigmoid(_dot(hn, wg_ref[:, 2 * D_MODEL:3 * D_MODEL])) * _dot(
        org_ref[...], wb_ref[RET_V + HG_V:RET_V + HG_V + RG_WIDTH, :])
    out_ref[...] = x_ref[...] + _dot(mixed.astype(BF16), wo_ref[...])


def _merge(hn, o_ret, o_hg, o_rg, x, wg, wb, wo, tm):
    rows = x.shape[0]
    rowspec = lambda width: pl.BlockSpec((tm, width), lambda i: (i, 0))
    return pl.pallas_call(
        _merge_kernel,
        grid=(rows // tm,),
        in_specs=[rowspec(D_MODEL), rowspec(RET_V), rowspec(HG_V), rowspec(RG_WIDTH), rowspec(D_MODEL),
                  _resident((D_MODEL, GATE_COLS)), _resident((RET_V + HG_V + RG_WIDTH, D_MODEL)),
                  _resident((D_MODEL, D_MODEL))],
        out_specs=rowspec(D_MODEL),
        out_shape=jax.ShapeDtypeStruct((rows, D_MODEL), F32),
        compiler_params=_cparams(("parallel",)),
        name="merge",
    )(hn, o_ret, o_hg, o_rg, x, wg, wb, wo)


def _ffn_kernel(x_ref, n2_ref, wup_ref, cw_ref, cb_ref, wdn_ref, buf0_ref, fn_ref, out_ref, buf_out_ref,
                abuf_ref, *, tb, final_norm):
    t = pl.program_id(1)
    hist = SUBLANES

    @pl.when(t == 0)
    def _():
        abuf_ref[0:hist, :] = jnp.zeros((hist, D_FF), F32)
        abuf_ref[hist - (FFN_CONV - 1):hist, :] = buf0_ref[0]

    x = x_ref[0]
    hn = _rms(x, n2_ref[...]).astype(BF16)
    abuf_ref[hist:hist + tb, :] = _dot(hn, wup_ref[:, 0:D_FF])
    gate = _dot(hn, wup_ref[:, D_FF:2 * D_FF])
    a = cb_ref[...]
    for j in range(FFN_CONV):
        a = a + cw_ref[j:j + 1, :] * abuf_ref[hist - (FFN_CONV - 1) + j:hist - (FFN_CONV - 1) + j + tb, :]
    y = x + _dot((_gelu(a) * gate).astype(BF16), wdn_ref[...])
    if final_norm:
        y = _rms(y, fn_ref[...])
    out_ref[0] = y

    @pl.when(t == pl.num_programs(1) - 1)
    def _():
        buf_out_ref[0] = abuf_ref[tb + hist - (FFN_CONV - 1):tb + hist, :]

    abuf_ref[0:hist, :] = abuf_ref[tb:tb + hist, :]


def _convffn(x, n2, wup, cw, cb, wdn, buf0, fn, tb, final_norm):
    b, t, _ = x.shape
    nt = t // tb
    return pl.pallas_call(
        functools.partial(_ffn_kernel, tb=tb, final_norm=final_norm),
        grid=(b, nt),
        in_specs=[
            pl.BlockSpec((1, tb, D_MODEL), lambda i, j: (i, j, 0)),
            _resident((1, D_MODEL)),
            _resident((D_MODEL, 2 * D_FF)),
            _resident((FFN_CONV, D_FF)),
            _resident((1, D_FF)),
            _resident((D_FF, D_MODEL)),
            pl.BlockSpec((1, FFN_CONV - 1, D_FF), lambda i, j: (i, 0, 0)),
            _resident((1, D_MODEL)),
        ],
        out_specs=[
            pl.BlockSpec((1, tb, D_MODEL), lambda i, j: (i, j, 0)),
            pl.BlockSpec((1, FFN_CONV - 1, D_FF), lambda i, j: (i, 0, 0)),
        ],
        out_shape=[
            jax.ShapeDtypeStruct((b, t, D_MODEL), F32),
            jax.ShapeDtypeStruct((b, FFN_CONV - 1, D_FF), F32),
        ],
        scratch_shapes=[pltpu.VMEM((tb + SUBLANES, D_FF), F32)],
        compiler_params=_cparams(("parallel", "arbitrary")),
        name="convffn",
    )(x, n2.reshape(1, D_MODEL), wup, cw, cb.reshape(1, D_FF), wdn, buf0, fn.reshape(1, D_MODEL))


def _rope_tables(t, pos0):
    half = RET_DK // 2
    inv = jnp.power(ROPE_BASE, -jnp.arange(half, dtype=F32) / half)
    ang = (jnp.arange(t, dtype=F32) + pos0)[:, None] * inv[None, :]
    cos, sin = jnp.cos(ang), jnp.sin(ang)
    return jnp.concatenate([cos, cos], axis=-1), jnp.concatenate([-sin, sin], axis=-1)


def _block_sizes(t):
    if t <= CHUNK:
        return t, t
    return TIME_BLOCK, CHUNK


def _trunk(x, pos0, r0, s0, h0, rgb0, ffb0, p):
    b, t, _ = x.shape
    tb, chunk = _block_sizes(t)
    rows = b * t
    tm = min(rows, ROW_BLOCK)
    cos2, sin2 = _rope_tables(t, pos0)
    depth = p["norm1_w"].shape[0]
    states = []
    x2d = x.reshape(rows, D_MODEL)
    for l in range(depth):
        o_ret, r_new, hn = _retention(
            x2d.reshape(b, t, D_MODEL), p["norm1_w"][l], p["w_ret"][l], cos2, sin2, r0[l], tb, chunk)
        hn2d = hn.reshape(rows, D_MODEL)
        o_hg, s_new = _hgrn(hn, p["w_hg"][l], p["hg_lb"], p["hg_norm_w"][l], s0[l], tb, chunk, l)
        o_rg, h_new, rgb_new = _rglru(
            hn, p["w_rg"][l], p["rg_conv_w"][l], p["rg_conv_b"][l], p["rg_w_r"][l], p["rg_b_r"][l],
            p["rg_w_i"][l], p["rg_b_i"][l], p["rg_lambda"][l], h0[l], rgb0[l], tb, pos0)
        x2d = _merge(hn2d, o_ret.reshape(rows, RET_V), o_hg.reshape(rows, HG_V), o_rg.reshape(rows, RG_WIDTH),
                     x2d, p["w_gate"][l], p["w_branch"][l], p["w_out"][l], tm)
        x3d, ffb_new = _convffn(
            x2d.reshape(b, t, D_MODEL), p["norm2_w"][l], p["w_up"][l], p["ffn_conv_w"][l], p["ffn_conv_b"][l],
            p["w_down"][l], ffb0[l], p["final_norm_w"], tb, l == depth - 1)
        x2d = x3d.reshape(rows, D_MODEL)
        states.append((r_new, s_new, h_new.reshape(b, RG_WIDTH), rgb_new, ffb_new))
    new_states = tuple(jnp.stack(st, axis=0) for st in zip(*states))
    return x2d.reshape(b, t, D_MODEL), new_states


def kernel(x_prompt, x_sample, state_ret, state_hgrn, state_rglru, cache_rg_conv, cache_ffn_conv,
           norm1_w, w_in, w_branch, w_out, rg_conv_w, rg_conv_b, rg_w_r, rg_b_r, rg_w_i, rg_b_i,
           rg_lambda, hg_lb, hg_norm_w, norm2_w, w_up, ffn_conv_w, ffn_conv_b, w_down, final_norm_w):
    depth = w_in.shape[0]
    batch = x_prompt.shape[0]
    bf = lambda a: a.astype(BF16)
    p = dict(
        norm1_w=norm1_w, norm2_w=norm2_w, final_norm_w=final_norm_w, hg_lb=hg_lb, hg_norm_w=hg_norm_w,
        w_ret=bf(w_in[:, :, 0:OFF_HG]), w_hg=bf(w_in[:, :, OFF_HG:OFF_RG]),
        w_rg=bf(w_in[:, :, OFF_RG:OFF_GATE]), w_gate=bf(w_in[:, :, OFF_GATE:OFF_GATE + GATE_COLS]),
        w_branch=bf(w_branch), w_out=bf(w_out), rg_conv_w=rg_conv_w, rg_conv_b=rg_conv_b,
        rg_w_r=bf(rg_w_r), rg_b_r=rg_b_r, rg_w_i=bf(rg_w_i), rg_b_i=rg_b_i, rg_lambda=rg_lambda,
        w_up=bf(w_up), ffn_conv_w=ffn_conv_w, ffn_conv_b=ffn_conv_b, w_down=bf(w_down),
    )
    zeros = lambda *shape: jnp.zeros((depth, batch) + shape, F32)
    y_p, (ret_p, hg_p, rgh_p, rgc_p, ffc_p) = _trunk(
        x_prompt, 0, zeros(RET_HEADS, RET_DK, RET_DV), zeros(HG_HEADS, HG_DK, HG_DV), zeros(RG_WIDTH),
        zeros(RG_CONV - 1, RG_WIDTH), zeros(FFN_CONV - 1, D_FF), p)
    y_s, (ret_s, hg_s, rgh_s, rgc_s, ffc_s) = _trunk(
        x_sample, PAST_LEN, state_ret, state_hgrn, state_rglru, cache_rg_conv, cache_ffn_conv, p)
    return (y_p, y_s, ret_p, ret_s, hg_p, hg_s, rgh_p, rgh_s, rgc_p, rgc_s, ffc_p, ffc_s)
```

```python
import functools
import math

import jax
import jax.numpy as jnp
from jax import lax
from jax.experimental import pallas as pl
from jax.experimental.pallas import tpu as pltpu

F32 = jnp.float32
BF16 = jnp.bfloat16

D_MODEL = 1024
PAST_LEN = 2048
CHUNK = 64
RET_HEADS, RET_DK, RET_DV = 4, 128, 256
RET_QK = RET_HEADS * RET_DK
RET_V = RET_HEADS * RET_DV
ROPE_BASE = 10000.0
HG_HEADS, HG_DK, HG_DV = 8, 128, 128
HG_K = HG_HEADS * HG_DK
HG_V = HG_HEADS * HG_DV
HG_MIN_F = 1e-6
HG_SAFE_DECAY = 150.0
RG_BLOCKS, RG_BLOCK = 5, 256
RG_WIDTH = RG_BLOCKS * RG_BLOCK
RG_CONV = 4
RG_C = 8.0
D_FF = 2816
FFN_CONV = 3
EPS = 1e-6

RET_COLS = 2 * RET_QK + 2 * RET_V
HG_COLS = 2 * HG_K + 2 * HG_V
RG_COLS = 2 * RG_WIDTH
GATE_COLS = 3 * D_MODEL
OFF_HG = RET_COLS
OFF_RG = OFF_HG + HG_COLS
OFF_GATE = OFF_RG + RG_COLS

VMEM_LIMIT_BYTES = 56 * 1024 * 1024
SUBLANES = 8
LANES = 128
TIME_BLOCK = 256
WIDE_TIME_BLOCK = 512
ROW_BLOCK = 512
GELU_C = math.sqrt(2.0 / math.pi)

LOG_GAMMA = tuple(math.log1p(-(2.0 ** (-5.0 - h))) for h in range(RET_HEADS))


def _cparams(sem):
    return pltpu.CompilerParams(dimension_semantics=sem, vmem_limit_bytes=VMEM_LIMIT_BYTES)


def _resident(shape):
    nd = len(shape)
    return pl.BlockSpec(shape, lambda *_: (0,) * nd, pipeline_mode=pl.Buffered(1))


def _sigmoid_pair(z):
    e = jnp.exp(-jnp.abs(z))
    d = pl.reciprocal(1.0 + e, approx=True)
    ed = e * d
    pos = z >= 0.0
    return jnp.where(pos, d, ed), jnp.where(pos, ed, d)


def _half_tanh_half(x):
    return jnp.tanh(0.5 * x)


def _sigmoid(x):
    return 0.5 * (1.0 + _half_tanh_half(x))


def _silu(x):
    return (0.5 * x) * (1.0 + _half_tanh_half(x))


def _gelu(x):
    inner = x * (GELU_C + (GELU_C * 0.044715) * (x * x))
    return (0.5 * x) * (1.0 + jnp.tanh(inner))


def _rms(x, w):
    ms = jnp.mean(x * x, axis=-1, keepdims=True)
    return x * lax.rsqrt(ms + EPS) * w


def _dot(a, b):
    return jnp.dot(a, b, preferred_element_type=F32)


def _dot_nt(a, b):
    return lax.dot_general(a, b, (((1,), (1,)), ((), ())), preferred_element_type=F32)


def _dot_tn(a, b):
    return lax.dot_general(a, b, (((0,), (0,)), ((), ())), preferred_element_type=F32)


def _ret_kernel(x_ref, n1_ref, w_ref, cos_ref, sin_ref, r0_ref, o_ref, r_ref, hn_ref, proj_ref, dec_ref, qd_ref,
                kd_ref, *, tb, chunk):
    t = pl.program_id(1)

    @pl.when(t == 0)
    def _():
        r_ref[...] = r0_ref[...]
        row = lax.broadcasted_iota(jnp.int32, (tb, tb), 0)
        col = lax.broadcasted_iota(jnp.int32, (tb, tb), 1)
        dist = jnp.abs(row - col).astype(F32)
        chunk_bits = chunk.bit_length() - 1
        later_chunk = col >= ((row >> chunk_bits) + 1) * chunk
        n = lax.broadcasted_iota(jnp.int32, (tb, RET_DK), 0).astype(F32)
        for h in range(RET_HEADS):
            lg = LOG_GAMMA[h]
            dec_ref[h] = jnp.where(later_chunk, 0.0, jnp.exp(dist * lg))
            qd_ref[h] = jnp.exp(lg * (n + 1.0))
            kd_ref[h] = jnp.exp(lg * (float(tb) - 1.0 - n))

    hn = _rms(x_ref[0], n1_ref[...]).astype(BF16)
    hn_ref[0] = hn
    proj_ref[...] = _dot(hn, w_ref[...])
    cos = cos_ref[...]
    sin = sin_ref[...]
    for h in range(RET_HEADS):
        q = proj_ref[:, h * RET_DK:(h + 1) * RET_DK]
        k = proj_ref[:, RET_QK + h * RET_DK:RET_QK + (h + 1) * RET_DK]
        vb = proj_ref[:, 2 * RET_QK + h * RET_DV:2 * RET_QK + (h + 1) * RET_DV].astype(BF16)
        q = q * cos + pltpu.roll(q, RET_DK // 2, 1) * sin
        k = (k * cos + pltpu.roll(k, RET_DK // 2, 1) * sin) * (RET_DK ** -0.5)
        scores = _dot_nt(q.astype(BF16), k.astype(BF16)) * dec_ref[h]
        r = r_ref[0, h]
        o = _dot(scores.astype(BF16), vb) + _dot((q * qd_ref[h]).astype(BF16), r.astype(BF16))
        r_ref[0, h] = math.exp(LOG_GAMMA[h] * tb) * r + _dot_tn((k * kd_ref[h]).astype(BF16), vb)
        o = o * lax.rsqrt(jnp.mean(o * o, axis=-1, keepdims=True) + EPS)
        g = proj_ref[:, 2 * RET_QK + RET_V + h * RET_DV:2 * RET_QK + RET_V + (h + 1) * RET_DV]
        o_ref[0, :, h * RET_DV:(h + 1) * RET_DV] = (o * _silu(g)).astype(o_ref.dtype)


def _retention(x, n1, w, cos2, sin2, r0, tb, chunk):
    b, t, _ = x.shape
    nt = t // tb
    return pl.pallas_call(
        functools.partial(_ret_kernel, tb=tb, chunk=chunk),
        grid=(b, nt),
        in_specs=[
            pl.BlockSpec((1, tb, D_MODEL), lambda i, j: (i, j, 0)),
            _resident((1, D_MODEL)),
            _resident((D_MODEL, RET_COLS)),
            pl.BlockSpec((tb, RET_DK), lambda i, j: (j, 0)),
            pl.BlockSpec((tb, RET_DK), lambda i, j: (j, 0)),
            pl.BlockSpec((1, RET_HEADS, RET_DK, RET_DV), lambda i, j: (i, 0, 0, 0)),
        ],
        out_specs=[
            pl.BlockSpec((1, tb, RET_V), lambda i, j: (i, j, 0)),
            pl.BlockSpec((1, RET_HEADS, RET_DK, RET_DV), lambda i, j: (i, 0, 0, 0)),
            pl.BlockSpec((1, tb, D_MODEL), lambda i, j: (i, j, 0)),
        ],
        out_shape=[
            jax.ShapeDtypeStruct((b, t, RET_V), BF16),
            jax.ShapeDtypeStruct((b, RET_HEADS, RET_DK, RET_DV), F32),
            jax.ShapeDtypeStruct((b, t, D_MODEL), BF16),
        ],
        scratch_shapes=[
            pltpu.VMEM((tb, RET_COLS), F32),
            pltpu.VMEM((RET_HEADS, tb, tb), F32),
            pltpu.VMEM((RET_HEADS, tb, RET_DK), F32),
            pltpu.VMEM((RET_HEADS, tb, RET_DK), F32),
        ],
        compiler_params=_cparams(("parallel", "arbitrary")),
        name="retention",
    )(x, n1.reshape(1, D_MODEL), w, cos2, sin2, r0)


def _hg_kernel(hn_ref, w_ref, lb_ref, nw_ref, s0_ref, o_ref, s_ref, proj_ref, st2_ref, kk_ref, bc_ref, qs_ref,
               acc_ref, *, chunk, n_chunks, layer, depth):
    t = pl.program_id(1)
    st_old = st2_ref.at[jnp.bitwise_and(t, 1)]
    st_new = st2_ref.at[jnp.bitwise_and(t + 1, 1)]

    @pl.when(t == 0)
    def _():
        for h in range(HG_HEADS):
            st2_ref[0, h] = s0_ref[0, h].T

    def project(lo, hi):
        proj_ref[:, lo:hi] = _dot(hn_ref[0], w_ref[:, lo:hi])

    project(HG_K, 2 * HG_K)
    project(0, HG_K)

    lrows = [lb_ref[d:d + 1, :] for d in range(depth)]
    mx = functools.reduce(jnp.maximum, lrows)
    ex = [jnp.exp(r - mx) for r in lrows]
    den = functools.reduce(lambda a, c: a + c, ex)
    lb = jnp.zeros_like(mx)
    for d in range(1, layer + 1):
        lb = lb + ex[d] / den
    one_m_lb = 1.0 - lb

    row = lax.broadcasted_iota(jnp.int32, (chunk, chunk), 0)
    col = lax.broadcasted_iota(jnp.int32, (chunk, chunk), 1)
    causal = row >= col
    tri = jnp.where(causal, 1.0, 0.0).astype(BF16)
    nw = nw_ref[...]

    def gates(c):
        sg, sgn = _sigmoid_pair(proj_ref[c * chunk:(c + 1) * chunk, HG_K:2 * HG_K])
        if layer == 0:
            fg, kk = sg, sgn
        else:
            fg, kk = lb + one_m_lb * sg, one_m_lb * sgn
        logf = jnp.log(jnp.maximum(fg, HG_MIN_F))
        g_hi = logf.astype(BF16)
        g_lo = (logf - g_hi.astype(F32)).astype(BF16)
        return kk, _dot(tri, g_hi) + _dot(tri, g_lo)

    steepest = []

    def prologue(c):
        rows = slice(c * chunk, (c + 1) * chunk)
        kk, bc = gates(c)
        total = bc[chunk - 1:chunk, :]
        steepest.append(total)
        mid = 0.5 * total
        bm = bc - mid
        qt = (_silu(proj_ref[rows, 0:HG_K]) * jnp.exp(bm)).astype(BF16)
        kt = (kk * jnp.exp(-bm)).astype(BF16)
        return dict(rows=rows, emid=jnp.exp(mid), qt=qt, kt=kt, src=st_old if c == 0 else st_new)

    def head_products(pro, h):
        sl = slice(h * HG_DK, (h + 1) * HG_DK)
        rows = pro["rows"]
        vb = proj_ref[rows, 2 * HG_K + h * HG_DV:2 * HG_K + (h + 1) * HG_DV].astype(BF16)
        qt, kt = pro["qt"][:, sl], pro["kt"][:, sl]
        return dict(vb=vb, qt=qt, emid=pro["emid"][:, sl], rows=rows, h=h, src=pro["src"],
                    scores=_dot_nt(qt, kt), kv=_dot_tn(vb, kt))

    def head_finish(hp):
        h, rows, emid = hp["h"], hp["rows"], hp["emid"]
        attn = jnp.where(causal, hp["scores"], 0.0)
        st_e = hp["src"][h] * emid
        o = _dot(attn.astype(BF16), hp["vb"]) + _dot_nt(hp["qt"], st_e.astype(BF16))
        st_new[h] = (st_e + hp["kv"]) * emid
        write_output(o, rows, h)

    def write_output(o, rows, h):
        o = o * lax.rsqrt(jnp.mean(o * o, axis=-1, keepdims=True) + EPS)
        gate = proj_ref[rows, 2 * HG_K + HG_V + h * HG_DV:2 * HG_K + HG_V + (h + 1) * HG_DV]
        o_ref[0, rows, h * HG_DV:(h + 1) * HG_DV] = (
            o * nw[:, h * HG_DV:(h + 1) * HG_DV] * _silu(gate)).astype(o_ref.dtype)

    pro = prologue(0)
    project(2 * HG_K, 2 * HG_K + HG_V)
    for c in range(n_chunks):
        products = [head_products(pro, h) for h in range(HG_HEADS)]
        if c == 0:
            project(2 * HG_K + HG_V, 2 * HG_K + 2 * HG_V)
        if c + 1 < n_chunks:
            pro = prologue(c + 1)
        for hp in products:
            head_finish(hp)

    steep = jnp.min(functools.reduce(jnp.minimum, steepest)) < -HG_SAFE_DECAY

    @pl.when(steep)
    def _():
        live_rows = lax.broadcasted_iota(jnp.int32, (chunk, HG_DV), 0)
        for c in range(n_chunks):
            rows = slice(c * chunk, (c + 1) * chunk)
            src = st_old if c == 0 else st_new
            kk, bc = gates(c)
            kk_ref[...] = kk
            bc_ref[...] = bc
            qs_ref[...] = _silu(proj_ref[rows, 0:HG_K])
            acc_ref[...] = jnp.zeros(acc_ref.shape, F32)

            def source_frame(s, carry):
                frame = pl.ds(c * chunk + s, 1)
                decay = jnp.exp(jnp.minimum(bc_ref[...] - bc_ref[pl.ds(s, 1), :], 0.0))
                term = qs_ref[...] * decay * kk_ref[pl.ds(s, 1), :]
                v_row = proj_ref[frame, 2 * HG_K:2 * HG_K + HG_V]
                for h in range(HG_HEADS):
                    sl = slice(h * HG_DK, (h + 1) * HG_DK)
                    weight = jnp.sum(term[:, sl], axis=1, keepdims=True)
                    acc_ref[:, sl] += jnp.where(live_rows >= s, weight * v_row[:, sl], 0.0)
                return carry

            lax.fori_loop(0, chunk, source_frame, 0)
            total = bc[chunk - 1:chunk, :]
            q_dec = (qs_ref[...] * jnp.exp(bc)).astype(BF16)
            k_dec = (kk * jnp.exp(total - bc)).astype(BF16)
            s_dec = jnp.exp(total)
            for h in range(HG_HEADS):
                sl = slice(h * HG_DK, (h + 1) * HG_DK)
                vb = proj_ref[rows, 2 * HG_K + h * HG_DV:2 * HG_K + (h + 1) * HG_DV].astype(BF16)
                st = src[h]
                o = acc_ref[:, sl] + _dot_nt(q_dec[:, sl], st.astype(BF16))
                st_new[h] = st * s_dec[:, sl] + _dot_tn(vb, k_dec[:, sl])
                write_output(o, rows, h)

    @pl.when(t == pl.num_programs(1) - 1)
    def _():
        for h in range(HG_HEADS):
            s_ref[0, h] = st_new[h].T


def _hgrn(hn, w, hg_lb, norm_w, s0, tb, chunk, layer):
    b, t, _ = hn.shape
    nt = t // tb
    depth = hg_lb.shape[0]
    return pl.pallas_call(
        functools.partial(_hg_kernel, chunk=chunk, n_chunks=tb // chunk, layer=layer, depth=depth),
        grid=(b, nt),
        in_specs=[
            pl.BlockSpec((1, tb, D_MODEL), lambda i, j: (i, j, 0)),
            _resident((D_MODEL, HG_COLS)),
            _resident((depth, HG_K)),
            _resident((1, HG_V)),
            pl.BlockSpec((1, HG_HEADS, HG_DK, HG_DV), lambda i, j: (i, 0, 0, 0)),
        ],
        out_specs=[
            pl.BlockSpec((1, tb, HG_V), lambda i, j: (i, j, 0)),
            pl.BlockSpec((1, HG_HEADS, HG_DK, HG_DV), lambda i, j: (i, 0, 0, 0)),
        ],
        out_shape=[
            jax.ShapeDtypeStruct((b, t, HG_V), BF16),
            jax.ShapeDtypeStruct((b, HG_HEADS, HG_DK, HG_DV), F32),
        ],
        scratch_shapes=[
            pltpu.VMEM((tb, HG_COLS), F32),
            pltpu.VMEM((2, HG_HEADS, HG_DV, HG_DK), F32),
            pltpu.VMEM((chunk, HG_K), F32),
            pltpu.VMEM((chunk, HG_K), F32),
            pltpu.VMEM((chunk, HG_K), F32),
            pltpu.VMEM((chunk, HG_V), F32),
        ],
        compiler_params=_cparams(("parallel", "arbitrary")),
        name="hgrn2",
    )(hn, w, hg_lb, norm_w.reshape(1, HG_V), s0)


def _rg_kernel(hn_ref, w_ref, cw_ref, cb_ref, wr_ref, br_ref, wi_ref, bi_ref, lam_ref, h0_ref, buf0_ref,
               o_ref, h_out_ref, buf_out_ref, gate_ref, ubuf_ref, a_ref, b_ref, perm_ref, tail_ref, cin_ref, h_ref,
               *, tb, first_pos_zero, interleaved):
    t = pl.program_id(1)
    seg = tb // SUBLANES
    step = SUBLANES if interleaved else 1
    hist = -(-(RG_CONV - 1) * step // SUBLANES) * SUBLANES
    n_hist = RG_CONV - 1

    @pl.when(t == 0)
    def _():
        tail_ref[...] = buf0_ref[0]
        h_ref[...] = h0_ref[0]
        if interleaved:
            row = lax.broadcasted_iota(jnp.int32, (tb, tb), 0)
            col = lax.broadcasted_iota(jnp.int32, (tb, tb), 1)
            frame_of = lambda p: jnp.bitwise_and(p, SUBLANES - 1) * seg + (p >> (SUBLANES.bit_length() - 1))
            perm_ref[0] = jnp.where(col == frame_of(row), 1.0, 0.0).astype(BF16)
            perm_ref[1] = jnp.where(row == frame_of(col), 1.0, 0.0).astype(BF16)

    hn = hn_ref[0]
    if interleaved:
        hn = _dot(perm_ref[0], hn).astype(BF16)
    gate_ref[...] = _dot(hn, w_ref[:, RG_WIDTH:2 * RG_WIDTH])
    gate_ref[...] = _gelu(gate_ref[...])

    ubuf_ref[hist:hist + tb, :] = _dot(hn, w_ref[:, 0:RG_WIDTH])
    if interleaved:
        row0 = lax.broadcasted_iota(jnp.int32, (SUBLANES, RG_WIDTH), 0) == 0
        for k in range(n_hist):
            src = hist + (seg - n_hist + k) * SUBLANES
            moved = pltpu.roll(ubuf_ref[src:src + SUBLANES, :], 1, 0)
            ubuf_ref[k * SUBLANES:(k + 1) * SUBLANES, :] = jnp.where(row0, tail_ref[k:k + 1, :], moved)
        for k in range(n_hist):
            last = hist + (seg - n_hist + k) * SUBLANES + SUBLANES - 1
            tail_ref[k:k + 1, :] = ubuf_ref[last:last + 1, :]
    else:
        ubuf_ref[hist - n_hist:hist, :] = tail_ref[...]
        tail_ref[...] = ubuf_ref[hist + tb - n_hist:hist + tb, :]

    lam = lam_ref[...]
    neg_half_c_sp = (-0.5 * RG_C) * (jnp.maximum(-lam, 0.0) + jnp.log1p(jnp.exp(-jnp.abs(lam))))
    first_row = lax.broadcasted_iota(jnp.int32, (tb, RG_BLOCK), 0) == 0

    for nb in range(RG_BLOCKS):
        sl = slice(nb * RG_BLOCK, (nb + 1) * RG_BLOCK)
        xc = cb_ref[:, sl]
        for j in range(RG_CONV):
            lo = hist - (n_hist - j) * step
            xc = xc + cw_ref[j:j + 1, sl] * ubuf_ref[lo:lo + tb, sl]
        xcb = xc.astype(BF16)
        tr = _half_tanh_half(_dot(xcb, wr_ref[nb]) + br_ref[:, sl])
        ti = _half_tanh_half(_dot(xcb, wi_ref[nb]) + bi_ref[:, sl])
        log_a = neg_half_c_sp[:, sl] * (1.0 + tr)
        th = jnp.tanh(log_a)
        m2 = -2.0 * th * pl.reciprocal(1.0 - th, approx=True)
        mult = jnp.where(m2 > 0.0, m2 * lax.rsqrt(m2), 0.0)
        if first_pos_zero:
            mult = jnp.where(jnp.logical_and(first_row, t == 0), 1.0, mult)
        a_ref[:, sl] = jnp.exp(log_a)
        b_ref[:, sl] = mult * ((0.5 * xc) * (1.0 + ti))

    if interleaved:
        def tile_scan(i, c):
            hloc, prod = c
            rows = pl.ds(pl.multiple_of(i * SUBLANES, SUBLANES), SUBLANES)
            a = a_ref[rows, :]
            hloc = a * hloc + b_ref[rows, :]
            prod = a * prod
            b_ref[rows, :] = hloc
            a_ref[rows, :] = prod
            return hloc, prod

        init = (jnp.zeros((SUBLANES, RG_WIDTH), F32), jnp.ones((SUBLANES, RG_WIDTH), F32))
        hloc, prod = lax.fori_loop(0, seg, tile_scan, init, unroll=4)
        carry = h_ref[...]
        for s in range(SUBLANES):
            cin_ref[s:s + 1, :] = carry
            carry = hloc[s:s + 1, :] + prod[s:s + 1, :] * carry
        h_ref[...] = carry
        cin = cin_ref[...]

        def tile_out(i, _):
            rows = pl.ds(pl.multiple_of(i * SUBLANES, SUBLANES), SUBLANES)
            h = b_ref[rows, :] + a_ref[rows, :] * cin
            b_ref[rows, :] = h * gate_ref[rows, :]
            return 0

        lax.fori_loop(0, seg, tile_out, 0, unroll=4)
        o_ref[0] = _dot(perm_ref[1], b_ref[...].astype(BF16)).astype(o_ref.dtype)
    else:
        def step_scan(i, h):
            h = a_ref[pl.ds(i, 1), :] * h + b_ref[pl.ds(i, 1), :]
            b_ref[pl.ds(i, 1), :] = h
            return h

        h_ref[...] = lax.fori_loop(0, tb, step_scan, h_ref[...], unroll=8)
        o_ref[0] = (b_ref[...] * gate_ref[...]).astype(o_ref.dtype)

    @pl.when(t == pl.num_programs(1) - 1)
    def _():
        h_out_ref[0] = h_ref[...]
        buf_out_ref[0] = tail_ref[...]


def _rglru(hn, w, cw, cb, wr, br, wi, bi, lam, h0, buf0, tb, pos0):
    b, t, _ = hn.shape
    nt = t // tb
    interleaved = tb % LANES == 0
    vec = lambda a: a.reshape(1, RG_WIDTH)
    return pl.pallas_call(
        functools.partial(_rg_kernel, tb=tb, first_pos_zero=(pos0 == 0), interleaved=interleaved),
        grid=(b, nt),
        in_specs=[
            pl.BlockSpec((1, tb, D_MODEL), lambda i, j: (i, j, 0)),
            _resident((D_MODEL, RG_COLS)),
            _resident((RG_CONV, RG_WIDTH)),
            _resident((1, RG_WIDTH)),
            _resident((RG_BLOCKS, RG_BLOCK, RG_BLOCK)),
            _resident((1, RG_WIDTH)),
            _resident((RG_BLOCKS, RG_BLOCK, RG_BLOCK)),
            _resident((1, RG_WIDTH)),
            _resident((1, RG_WIDTH)),
            pl.BlockSpec((1, 1, RG_WIDTH), lambda i, j: (i, 0, 0)),
            pl.BlockSpec((1, RG_CONV - 1, RG_WIDTH), lambda i, j: (i, 0, 0)),
        ],
        out_specs=[
            pl.BlockSpec((1, tb, RG_WIDTH), lambda i, j: (i, j, 0)),
            pl.BlockSpec((1, 1, RG_WIDTH), lambda i, j: (i, 0, 0)),
            pl.BlockSpec((1, RG_CONV - 1, RG_WIDTH), lambda i, j: (i, 0, 0)),
        ],
        out_shape=[
            jax.ShapeDtypeStruct((b, t, RG_WIDTH), BF16),
            jax.ShapeDtypeStruct((b, 1, RG_WIDTH), F32),
            jax.ShapeDtypeStruct((b, RG_CONV - 1, RG_WIDTH), F32),
        ],
        scratch_shapes=[
            pltpu.VMEM((tb, RG_WIDTH), F32),
            pltpu.VMEM((tb + (RG_CONV - 1) * SUBLANES, RG_WIDTH), F32),
            pltpu.VMEM((tb, RG_WIDTH), F32),
            pltpu.VMEM((tb, RG_WIDTH), F32),
            pltpu.VMEM((2, tb, tb), BF16),
            pltpu.VMEM((RG_CONV - 1, RG_WIDTH), F32),
            pltpu.VMEM((SUBLANES, RG_WIDTH), F32),
            pltpu.VMEM((1, RG_WIDTH), F32),
        ],
        compiler_params=_cparams(("parallel", "arbitrary")),
        name="rglru",
    )(hn, w, cw, vec(cb), wr, vec(br), wi, vec(bi), vec(lam), h0.reshape(b, 1, RG_WIDTH), buf0)


def _merge_kernel(hn_ref, oret_ref, ohg_ref, org_ref, x_ref, wg_ref, wb_ref, wo_ref, out_ref):
    hn = hn_ref[...]
    mixed = _sigmoid(_dot(hn, wg_ref[:, 0:D_MODEL])) * _dot(oret_ref[...], wb_ref[0:RET_V, :])
    mixed = mixed + _sigmoid(_dot(hn, wg_ref[:, D_MODEL:2 * D_MODEL])) * _dot(
        ohg_ref[...], wb_ref[RET_V:RET_V + HG_V, :])
    mixed = mixed + _sigmoid(_dot(hn, wg_ref[:, 2 * D_MODEL:3 * D_MODEL])) * _dot(
        org_ref[...], wb_ref[RET_V + HG_V:RET_V + HG_V + RG_WIDTH, :])
    out_ref[...] = x_ref[...] + _dot(mixed.astype(BF16), wo_ref[...])


def _merge(hn, o_ret, o_hg, o_rg, x, wg, wb, wo, tm):
    rows = x.shape[0]
    rowspec = lambda width: pl.BlockSpec((tm, width), lambda i: (i, 0))
    return pl.pallas_call(
        _merge_kernel,
        grid=(rows // tm,),
        in_specs=[rowspec(D_MODEL), rowspec(RET_V), rowspec(HG_V), rowspec(RG_WIDTH), rowspec(D_MODEL),
                  _resident((D_MODEL, GATE_COLS)), _resident((RET_V + HG_V + RG_WIDTH, D_MODEL)),
                  _resident((D_MODEL, D_MODEL))],
        out_specs=rowspec(D_MODEL),
        out_shape=jax.ShapeDtypeStruct((rows, D_MODEL), F32),
        compiler_params=_cparams(("parallel",)),
        name="merge",
    )(hn, o_ret, o_hg, o_rg, x, wg, wb, wo)


def _ffn_kernel(x_ref, n2_ref, wup_ref, cw_ref, cb_ref, wdn_ref, buf0_ref, fn_ref, out_ref, buf_out_ref,
                abuf_ref, *, tb, final_norm):
    t = pl.program_id(1)
    hist = SUBLANES

    @pl.when(t == 0)
    def _():
        abuf_ref[0:hist, :] = jnp.zeros((hist, D_FF), F32)
        abuf_ref[hist - (FFN_CONV - 1):hist, :] = buf0_ref[0]

    x = x_ref[0]
    hn = _rms(x, n2_ref[...]).astype(BF16)
    abuf_ref[hist:hist + tb, :] = _dot(hn, wup_ref[:, 0:D_FF])
    gate = _dot(hn, wup_ref[:, D_FF:2 * D_FF])
    a = cb_ref[...]
    for j in range(FFN_CONV):
        a = a + cw_ref[j:j + 1, :] * abuf_ref[hist - (FFN_CONV - 1) + j:hist - (FFN_CONV - 1) + j + tb, :]
    y = x + _dot((_gelu(a) * gate).astype(BF16), wdn_ref[...])
    if final_norm:
        y = _rms(y, fn_ref[...])
    out_ref[0] = y

    @pl.when(t == pl.num_programs(1) - 1)
    def _():
        buf_out_ref[0] = abuf_ref[tb + hist - (FFN_CONV - 1):tb + hist, :]

    abuf_ref[0:hist, :] = abuf_ref[tb:tb + hist, :]


def _convffn(x, n2, wup, cw, cb, wdn, buf0, fn, tb, final_norm):
    b, t, _ = x.shape
    nt = t // tb
    return pl.pallas_call(
        functools.partial(_ffn_kernel, tb=tb, final_norm=final_norm),
        grid=(b, nt),
        in_specs=[
            pl.BlockSpec((1, tb, D_MODEL), lambda i, j: (i, j, 0)),
            _resident((1, D_MODEL)),
            _resident((D_MODEL, 2 * D_FF)),
            _resident((FFN_CONV, D_FF)),
            _resident((1, D_FF)),
            _resident((D_FF, D_MODEL)),
            pl.BlockSpec((1, FFN_CONV - 1, D_FF), lambda i, j: (i, 0, 0)),
            _resident((1, D_MODEL)),
        ],
        out_specs=[
            pl.BlockSpec((1, tb, D_MODEL), lambda i, j: (i, j, 0)),
            pl.BlockSpec((1, FFN_CONV - 1, D_FF), lambda i, j: (i, 0, 0)),
        ],
        out_shape=[
            jax.ShapeDtypeStruct((b, t, D_MODEL), F32),
            jax.ShapeDtypeStruct((b, FFN_CONV - 1, D_FF), F32),
        ],
        scratch_shapes=[pltpu.VMEM((tb + SUBLANES, D_FF), F32)],
        compiler_params=_cparams(("parallel", "arbitrary")),
        name="convffn",
    )(x, n2.reshape(1, D_MODEL), wup, cw, cb.reshape(1, D_FF), wdn, buf0, fn.reshape(1, D_MODEL))


def _rope_tables(t, pos0):
    half = RET_DK // 2
    inv = jnp.power(ROPE_BASE, -jnp.arange(half, dtype=F32) / half)
    ang = (jnp.arange(t, dtype=F32) + pos0)[:, None] * inv[None, :]
    cos, sin = jnp.cos(ang), jnp.sin(ang)
    return jnp.concatenate([cos, cos], axis=-1), jnp.concatenate([-sin, sin], axis=-1)


def _block_sizes(t):
    if t <= CHUNK:
        return t, t, t
    return TIME_BLOCK, WIDE_TIME_BLOCK, CHUNK


def _trunk(x, pos0, r0, s0, h0, rgb0, ffb0, p):
    b, t, _ = x.shape
    tb, tb_wide, chunk = _block_sizes(t)
    rows = b * t
    tm = min(rows, ROW_BLOCK)
    cos2, sin2 = _rope_tables(t, pos0)
    depth = p["norm1_w"].shape[0]
    states = []
    x2d = x.reshape(rows, D_MODEL)
    for l in range(depth):
        o_ret, r_new, hn = _retention(
            x2d.reshape(b, t, D_MODEL), p["norm1_w"][l], p["w_ret"][l], cos2, sin2, r0[l], tb, chunk)
        hn2d = hn.reshape(rows, D_MODEL)
        o_hg, s_new = _hgrn(hn, p["w_hg"][l], p["hg_lb"], p["hg_norm_w"][l], s0[l], tb_wide, chunk, l)
        o_rg, h_new, rgb_new = _rglru(
            hn, p["w_rg"][l], p["rg_conv_w"][l], p["rg_conv_b"][l], p["rg_w_r"][l], p["rg_b_r"][l],
            p["rg_w_i"][l], p["rg_b_i"][l], p["rg_lambda"][l], h0[l], rgb0[l], tb, pos0)
        x2d = _merge(hn2d, o_ret.reshape(rows, RET_V), o_hg.reshape(rows, HG_V), o_rg.reshape(rows, RG_WIDTH),
                     x2d, p["w_gate"][l], p["w_branch"][l], p["w_out"][l], tm)
        x3d, ffb_new = _convffn(
            x2d.reshape(b, t, D_MODEL), p["norm2_w"][l], p["w_up"][l], p["ffn_conv_w"][l], p["ffn_conv_b"][l],
            p["w_down"][l], ffb0[l], p["final_norm_w"], tb_wide, l == depth - 1)
        x2d = x3d.reshape(rows, D_MODEL)
        states.append((r_new, s_new, h_new.reshape(b, RG_WIDTH), rgb_new, ffb_new))
    new_states = tuple(jnp.stack(st, axis=0) for st in zip(*states))
    return x2d.reshape(b, t, D_MODEL), new_states


def kernel(x_prompt, x_sample, state_ret, state_hgrn, state_rglru, cache_rg_conv, cache_ffn_conv,
           norm1_w, w_in, w_branch, w_out, rg_conv_w, rg_conv_b, rg_w_r, rg_b_r, rg_w_i, rg_b_i,
           rg_lambda, hg_lb, hg_norm_w, norm2_w, w_up, ffn_conv_w, ffn_conv_b, w_down, final_norm_w):
    depth = w_in.shape[0]
    batch = x_prompt.shape[0]
    bf = lambda a: a.astype(BF16)
    p = dict(
        norm1_w=norm1_w, norm2_w=norm2_w, final_norm_w=final_norm_w, hg_lb=hg_lb, hg_norm_w=hg_norm_w,
        w_ret=bf(w_in[:, :, 0:OFF_HG]), w_hg=bf(w_in[:, :, OFF_HG:OFF_RG]),
        w_rg=bf(w_in[:, :, OFF_RG:OFF_GATE]), w_gate=bf(w_in[:, :, OFF_GATE:OFF_GATE + GATE_COLS]),
        w_branch=bf(w_branch), w_out=bf(w_out), rg_conv_w=rg_conv_w, rg_conv_b=rg_conv_b,
        rg_w_r=bf(rg_w_r), rg_b_r=rg_b_r, rg_w_i=bf(rg_w_i), rg_b_i=rg_b_i, rg_lambda=rg_lambda,
        w_up=bf(w_up), ffn_conv_w=ffn_conv_w, ffn_conv_b=ffn_conv_b, w_down=bf(w_down),
    )
    zeros = lambda *shape: jnp.zeros((depth, batch) + shape, F32)
    y_p, (ret_p, hg_p, rgh_p, rgc_p, ffc_p) = _trunk(
        x_prompt, 0, zeros(RET_HEADS, RET_DK, RET_DV), zeros(HG_HEADS, HG_DK, HG_DV), zeros(RG_WIDTH),
        zeros(RG_CONV - 1, RG_WIDTH), zeros(FFN_CONV - 1, D_FF), p)
    y_s, (ret_s, hg_s, rgh_s, rgc_s, ffc_s) = _trunk(
        x_sample, PAST_LEN, state_ret, state_hgrn, state_rglru, cache_rg_conv, cache_ffn_conv, p)
    return (y_p, y_s, ret_p, ret_s, hg_p, hg_s, rgh_p, rgh_s, rgc_p, rgc_s, ffc_p, ffc_s)
```

```python
import functools
import math

import jax
import jax.numpy as jnp
from jax import lax
from jax.experimental import pallas as pl
from jax.experimental.pallas import tpu as pltpu

F32 = jnp.float32
BF16 = jnp.bfloat16

D_MODEL = 1024
PAST_LEN = 2048
CHUNK = 64
RET_HEADS, RET_DK, RET_DV = 4, 128, 256
RET_QK = RET_HEADS * RET_DK
RET_V = RET_HEADS * RET_DV
ROPE_BASE = 10000.0
HG_HEADS, HG_DK, HG_DV = 8, 128, 128
HG_K = HG_HEADS * HG_DK
HG_V = HG_HEADS * HG_DV
HG_MIN_F = 1e-6
HG_SAFE_DECAY = 150.0
RG_BLOCKS, RG_BLOCK = 5, 256
RG_WIDTH = RG_BLOCKS * RG_BLOCK
RG_CONV = 4
RG_C = 8.0
D_FF = 2816
FFN_CONV = 3
EPS = 1e-6

RET_COLS = 2 * RET_QK + 2 * RET_V
HG_COLS = 2 * HG_K + 2 * HG_V
RG_COLS = 2 * RG_WIDTH
GATE_COLS = 3 * D_MODEL
OFF_HG = RET_COLS
OFF_RG = OFF_HG + HG_COLS
OFF_GATE = OFF_RG + RG_COLS

VMEM_LIMIT_BYTES = 56 * 1024 * 1024
SUBLANES = 8
LANES = 128
TIME_BLOCK = 256
WIDE_TIME_BLOCK = 512
ROW_BLOCK = 512
GELU_C = math.sqrt(2.0 / math.pi)

LOG_GAMMA = tuple(math.log1p(-(2.0 ** (-5.0 - h))) for h in range(RET_HEADS))


def _cparams(sem):
    return pltpu.CompilerParams(dimension_semantics=sem, vmem_limit_bytes=VMEM_LIMIT_BYTES)


def _resident(shape):
    nd = len(shape)
    return pl.BlockSpec(shape, lambda *_: (0,) * nd, pipeline_mode=pl.Buffered(1))


def _sigmoid_pair(z):
    e = jnp.exp(-jnp.abs(z))
    d = 1.0 / (1.0 + e)
    ed = e * d
    pos = z >= 0.0
    return jnp.where(pos, d, ed), jnp.where(pos, ed, d)


def _half_tanh_half(x):
    return jnp.tanh(0.5 * x)


def _sigmoid(x):
    return 0.5 * (1.0 + _half_tanh_half(x))


def _silu(x):
    return (0.5 * x) * (1.0 + _half_tanh_half(x))


def _gelu(x):
    inner = x * (GELU_C + (GELU_C * 0.044715) * (x * x))
    return (0.5 * x) * (1.0 + jnp.tanh(inner))


def _rms(x, w):
    ms = jnp.mean(x * x, axis=-1, keepdims=True)
    return x * lax.rsqrt(ms + EPS) * w


def _dot(a, b):
    return jnp.dot(a, b, preferred_element_type=F32)


def _dot_nt(a, b):
    return lax.dot_general(a, b, (((1,), (1,)), ((), ())), preferred_element_type=F32)


def _dot_tn(a, b):
    return lax.dot_general(a, b, (((0,), (0,)), ((), ())), preferred_element_type=F32)


def _ret_kernel(x_ref, n1_ref, w_ref, cos_ref, sin_ref, r0_ref, o_ref, r_ref, hn_ref, proj_ref, dec_ref, qd_ref,
                kd_ref, *, tb, chunk):
    t = pl.program_id(1)

    @pl.when(t == 0)
    def _():
        r_ref[...] = r0_ref[...]
        row = lax.broadcasted_iota(jnp.int32, (tb, tb), 0)
        col = lax.broadcasted_iota(jnp.int32, (tb, tb), 1)
        dist = jnp.abs(row - col).astype(F32)
        chunk_bits = chunk.bit_length() - 1
        later_chunk = col >= ((row >> chunk_bits) + 1) * chunk
        n = lax.broadcasted_iota(jnp.int32, (tb, RET_DK), 0).astype(F32)
        for h in range(RET_HEADS):
            lg = LOG_GAMMA[h]
            dec_ref[h] = jnp.where(later_chunk, 0.0, jnp.exp(dist * lg))
            qd_ref[h] = jnp.exp(lg * (n + 1.0))
            kd_ref[h] = jnp.exp(lg * (float(tb) - 1.0 - n))

    hn = _rms(x_ref[0], n1_ref[...]).astype(BF16)
    hn_ref[0] = hn
    proj_ref[...] = _dot(hn, w_ref[...])
    cos = cos_ref[...]
    sin = sin_ref[...]
    for h in range(RET_HEADS):
        q = proj_ref[:, h * RET_DK:(h + 1) * RET_DK]
        k = proj_ref[:, RET_QK + h * RET_DK:RET_QK + (h + 1) * RET_DK]
        vb = proj_ref[:, 2 * RET_QK + h * RET_DV:2 * RET_QK + (h + 1) * RET_DV].astype(BF16)
        q = q * cos + pltpu.roll(q, RET_DK // 2, 1) * sin
        k = (k * cos + pltpu.roll(k, RET_DK // 2, 1) * sin) * (RET_DK ** -0.5)
        scores = _dot_nt(q.astype(BF16), k.astype(BF16)) * dec_ref[h]
        r = r_ref[0, h]
        o = _dot(scores.astype(BF16), vb) + _dot((q * qd_ref[h]).astype(BF16), r.astype(BF16))
        r_ref[0, h] = math.exp(LOG_GAMMA[h] * tb) * r + _dot_tn((k * kd_ref[h]).astype(BF16), vb)
        o = o * lax.rsqrt(jnp.mean(o * o, axis=-1, keepdims=True) + EPS)
        g = proj_ref[:, 2 * RET_QK + RET_V + h * RET_DV:2 * RET_QK + RET_V + (h + 1) * RET_DV]
        o_ref[0, :, h * RET_DV:(h + 1) * RET_DV] = (o * _silu(g)).astype(o_ref.dtype)


def _retention(x, n1, w, cos2, sin2, r0, tb, chunk):
    b, t, _ = x.shape
    nt = t // tb
    return pl.pallas_call(
        functools.partial(_ret_kernel, tb=tb, chunk=chunk),
        grid=(b, nt),
        in_specs=[
            pl.BlockSpec((1, tb, D_MODEL), lambda i, j: (i, j, 0)),
            _resident((1, D_MODEL)),
            _resident((D_MODEL, RET_COLS)),
            pl.BlockSpec((tb, RET_DK), lambda i, j: (j, 0)),
            pl.BlockSpec((tb, RET_DK), lambda i, j: (j, 0)),
            pl.BlockSpec((1, RET_HEADS, RET_DK, RET_DV), lambda i, j: (i, 0, 0, 0)),
        ],
        out_specs=[
            pl.BlockSpec((1, tb, RET_V), lambda i, j: (i, j, 0)),
            pl.BlockSpec((1, RET_HEADS, RET_DK, RET_DV), lambda i, j: (i, 0, 0, 0)),
            pl.BlockSpec((1, tb, D_MODEL), lambda i, j: (i, j, 0)),
        ],
        out_shape=[
            jax.ShapeDtypeStruct((b, t, RET_V), BF16),
            jax.ShapeDtypeStruct((b, RET_HEADS, RET_DK, RET_DV), F32),
            jax.ShapeDtypeStruct((b, t, D_MODEL), BF16),
        ],
        scratch_shapes=[
            pltpu.VMEM((tb, RET_COLS), F32),
            pltpu.VMEM((RET_HEADS, tb, tb), F32),
            pltpu.VMEM((RET_HEADS, tb, RET_DK), F32),
            pltpu.VMEM((RET_HEADS, tb, RET_DK), F32),
        ],
        compiler_params=_cparams(("parallel", "arbitrary")),
        name="retention",
    )(x, n1.reshape(1, D_MODEL), w, cos2, sin2, r0)


def _hg_kernel(hn_ref, w_ref, lb_ref, nw_ref, s0_ref, o_ref, s_ref, proj_ref, st2_ref, kk_ref, bc_ref, qs_ref,
               acc_ref, *, chunk, n_chunks, layer, depth):
    t = pl.program_id(1)
    st_old = st2_ref.at[jnp.bitwise_and(t, 1)]
    st_new = st2_ref.at[jnp.bitwise_and(t + 1, 1)]

    @pl.when(t == 0)
    def _():
        for h in range(HG_HEADS):
            st2_ref[0, h] = s0_ref[0, h].T

    def project(lo, hi):
        proj_ref[:, lo:hi] = _dot(hn_ref[0], w_ref[:, lo:hi])

    project(HG_K, 2 * HG_K)
    project(0, HG_K)

    lrows = [lb_ref[d:d + 1, :] for d in range(depth)]
    mx = functools.reduce(jnp.maximum, lrows)
    ex = [jnp.exp(r - mx) for r in lrows]
    den = functools.reduce(lambda a, c: a + c, ex)
    lb = jnp.zeros_like(mx)
    for d in range(1, layer + 1):
        lb = lb + ex[d] / den
    one_m_lb = 1.0 - lb

    row = lax.broadcasted_iota(jnp.int32, (chunk, chunk), 0)
    col = lax.broadcasted_iota(jnp.int32, (chunk, chunk), 1)
    causal = row >= col
    tri = jnp.where(causal, 1.0, 0.0).astype(BF16)
    nw = nw_ref[...]

    def gates(c):
        sg, sgn = _sigmoid_pair(proj_ref[c * chunk:(c + 1) * chunk, HG_K:2 * HG_K])
        if layer == 0:
            fg, kk = sg, sgn
        else:
            fg, kk = lb + one_m_lb * sg, one_m_lb * sgn
        logf = jnp.log(jnp.maximum(fg, HG_MIN_F))
        g_hi = logf.astype(BF16)
        g_lo = (logf - g_hi.astype(F32)).astype(BF16)
        return kk, _dot(tri, g_hi) + _dot(tri, g_lo)

    steepest = []

    def prologue(c):
        rows = slice(c * chunk, (c + 1) * chunk)
        kk, bc = gates(c)
        total = bc[chunk - 1:chunk, :]
        steepest.append(total)
        mid = 0.5 * total
        bm = bc - mid
        qt = (_silu(proj_ref[rows, 0:HG_K]) * jnp.exp(bm)).astype(BF16)
        kt = (kk * jnp.exp(-bm)).astype(BF16)
        return dict(rows=rows, emid=jnp.exp(mid), qt=qt, kt=kt, src=st_old if c == 0 else st_new)

    def head_products(pro, h):
        sl = slice(h * HG_DK, (h + 1) * HG_DK)
        rows = pro["rows"]
        vb = proj_ref[rows, 2 * HG_K + h * HG_DV:2 * HG_K + (h + 1) * HG_DV].astype(BF16)
        qt, kt = pro["qt"][:, sl], pro["kt"][:, sl]
        return dict(vb=vb, qt=qt, emid=pro["emid"][:, sl], rows=rows, h=h, src=pro["src"],
                    scores=_dot_nt(qt, kt), kv=_dot_tn(vb, kt))

    def head_finish(hp):
        h, rows, emid = hp["h"], hp["rows"], hp["emid"]
        attn = jnp.where(causal, hp["scores"], 0.0)
        st_e = hp["src"][h] * emid
        o = _dot(attn.astype(BF16), hp["vb"]) + _dot_nt(hp["qt"], st_e.astype(BF16))
        st_new[h] = (st_e + hp["kv"]) * emid
        write_output(o, rows, h)

    def write_output(o, rows, h):
        o = o * lax.rsqrt(jnp.mean(o * o, axis=-1, keepdims=True) + EPS)
        gate = proj_ref[rows, 2 * HG_K + HG_V + h * HG_DV:2 * HG_K + HG_V + (h + 1) * HG_DV]
        o_ref[0, rows, h * HG_DV:(h + 1) * HG_DV] = (
            o * nw[:, h * HG_DV:(h + 1) * HG_DV] * _silu(gate)).astype(o_ref.dtype)

    pro = prologue(0)
    project(2 * HG_K, 2 * HG_K + HG_V)
    for c in range(n_chunks):
        products = [head_products(pro, h) for h in range(HG_HEADS)]
        if c == 0:
            project(2 * HG_K + HG_V, 2 * HG_K + 2 * HG_V)
        if c + 1 < n_chunks:
            pro = prologue(c + 1)
        for hp in products:
            head_finish(hp)

    steep = jnp.min(functools.reduce(jnp.minimum, steepest)) < -HG_SAFE_DECAY

    @pl.when(steep)
    def _():
        live_rows = lax.broadcasted_iota(jnp.int32, (chunk, HG_DV), 0)
        for c in range(n_chunks):
            rows = slice(c * chunk, (c + 1) * chunk)
            src = st_old if c == 0 else st_new
            kk, bc = gates(c)
            kk_ref[...] = kk
            bc_ref[...] = bc
            qs_ref[...] = _silu(proj_ref[rows, 0:HG_K])
            acc_ref[...] = jnp.zeros(acc_ref.shape, F32)

            def source_frame(s, carry):
                frame = pl.ds(c * chunk + s, 1)
                decay = jnp.exp(jnp.minimum(bc_ref[...] - bc_ref[pl.ds(s, 1), :], 0.0))
                term = qs_ref[...] * decay * kk_ref[pl.ds(s, 1), :]
                v_row = proj_ref[frame, 2 * HG_K:2 * HG_K + HG_V]
                for h in range(HG_HEADS):
                    sl = slice(h * HG_DK, (h + 1) * HG_DK)
                    weight = jnp.sum(term[:, sl], axis=1, keepdims=True)
                    acc_ref[:, sl] += jnp.where(live_rows >= s, weight * v_row[:, sl], 0.0)
                return carry

            lax.fori_loop(0, chunk, source_frame, 0)
            total = bc[chunk - 1:chunk, :]
            q_dec = (qs_ref[...] * jnp.exp(bc)).astype(BF16)
            k_dec = (kk * jnp.exp(total - bc)).astype(BF16)
            s_dec = jnp.exp(total)
            for h in range(HG_HEADS):
                sl = slice(h * HG_DK, (h + 1) * HG_DK)
                vb = proj_ref[rows, 2 * HG_K + h * HG_DV:2 * HG_K + (h + 1) * HG_DV].astype(BF16)
                st = src[h]
                o = acc_ref[:, sl] + _dot_nt(q_dec[:, sl], st.astype(BF16))
                st_new[h] = st * s_dec[:, sl] + _dot_tn(vb, k_dec[:, sl])
                write_output(o, rows, h)

    @pl.when(t == pl.num_programs(1) - 1)
    def _():
        for h in range(HG_HEADS):
            s_ref[0, h] = st_new[h].T


def _hgrn(hn, w, hg_lb, norm_w, s0, tb, chunk, layer):
    b, t, _ = hn.shape
    nt = t // tb
    depth = hg_lb.shape[0]
    return pl.pallas_call(
        functools.partial(_hg_kernel, chunk=chunk, n_chunks=tb // chunk, layer=layer, depth=depth),
        grid=(b, nt),
        in_specs=[
            pl.BlockSpec((1, tb, D_MODEL), lambda i, j: (i, j, 0)),
            _resident((D_MODEL, HG_COLS)),
            _resident((depth, HG_K)),
            _resident((1, HG_V)),
            pl.BlockSpec((1, HG_HEADS, HG_DK, HG_DV), lambda i, j: (i, 0, 0, 0)),
        ],
        out_specs=[
            pl.BlockSpec((1, tb, HG_V), lambda i, j: (i, j, 0)),
            pl.BlockSpec((1, HG_HEADS, HG_DK, HG_DV), lambda i, j: (i, 0, 0, 0)),
        ],
        out_shape=[
            jax.ShapeDtypeStruct((b, t, HG_V), BF16),
            jax.ShapeDtypeStruct((b, HG_HEADS, HG_DK, HG_DV), F32),
        ],
        scratch_shapes=[
            pltpu.VMEM((tb, HG_COLS), F32),
            pltpu.VMEM((2, HG_HEADS, HG_DV, HG_DK), F32),
            pltpu.VMEM((chunk, HG_K), F32),
            pltpu.VMEM((chunk, HG_K), F32),
            pltpu.VMEM((chunk, HG_K), F32),
            pltpu.VMEM((chunk, HG_V), F32),
        ],
        compiler_params=_cparams(("parallel", "arbitrary")),
        name="hgrn2",
    )(hn, w, hg_lb, norm_w.reshape(1, HG_V), s0)


def _rg_kernel(hn_ref, w_ref, cw_ref, cb_ref, wr_ref, br_ref, wi_ref, bi_ref, lam_ref, h0_ref, buf0_ref,
               o_ref, h_out_ref, buf_out_ref, gate_ref, ubuf_ref, a_ref, b_ref, perm_ref, tail_ref, cin_ref, h_ref,
               *, tb, first_pos_zero, interleaved):
    t = pl.program_id(1)
    seg = tb // SUBLANES
    step = SUBLANES if interleaved else 1
    hist = -(-(RG_CONV - 1) * step // SUBLANES) * SUBLANES
    n_hist = RG_CONV - 1

    @pl.when(t == 0)
    def _():
        tail_ref[...] = buf0_ref[0]
        h_ref[...] = h0_ref[0]
        if interleaved:
            row = lax.broadcasted_iota(jnp.int32, (tb, tb), 0)
            col = lax.broadcasted_iota(jnp.int32, (tb, tb), 1)
            frame_of = lambda p: jnp.bitwise_and(p, SUBLANES - 1) * seg + (p >> (SUBLANES.bit_length() - 1))
            perm_ref[0] = jnp.where(col == frame_of(row), 1.0, 0.0).astype(BF16)
            perm_ref[1] = jnp.where(row == frame_of(col), 1.0, 0.0).astype(BF16)

    hn = hn_ref[0]
    if interleaved:
        hn = _dot(perm_ref[0], hn).astype(BF16)
    gate_ref[...] = _dot(hn, w_ref[:, RG_WIDTH:2 * RG_WIDTH])
    gate_ref[...] = _gelu(gate_ref[...])

    ubuf_ref[hist:hist + tb, :] = _dot(hn, w_ref[:, 0:RG_WIDTH])
    if interleaved:
        row0 = lax.broadcasted_iota(jnp.int32, (SUBLANES, RG_WIDTH), 0) == 0
        for k in range(n_hist):
            src = hist + (seg - n_hist + k) * SUBLANES
            moved = pltpu.roll(ubuf_ref[src:src + SUBLANES, :], 1, 0)
            ubuf_ref[k * SUBLANES:(k + 1) * SUBLANES, :] = jnp.where(row0, tail_ref[k:k + 1, :], moved)
        for k in range(n_hist):
            last = hist + (seg - n_hist + k) * SUBLANES + SUBLANES - 1
            tail_ref[k:k + 1, :] = ubuf_ref[last:last + 1, :]
    else:
        ubuf_ref[hist - n_hist:hist, :] = tail_ref[...]
        tail_ref[...] = ubuf_ref[hist + tb - n_hist:hist + tb, :]

    lam = lam_ref[...]
    neg_half_c_sp = (-0.5 * RG_C) * (jnp.maximum(-lam, 0.0) + jnp.log1p(jnp.exp(-jnp.abs(lam))))
    first_row = lax.broadcasted_iota(jnp.int32, (tb, RG_BLOCK), 0) == 0

    for nb in range(RG_BLOCKS):
        sl = slice(nb * RG_BLOCK, (nb + 1) * RG_BLOCK)
        xc = cb_ref[:, sl]
        for j in range(RG_CONV):
            lo = hist - (n_hist - j) * step
            xc = xc + cw_ref[j:j + 1, sl] * ubuf_ref[lo:lo + tb, sl]
        xcb = xc.astype(BF16)
        tr = _half_tanh_half(_dot(xcb, wr_ref[nb]) + br_ref[:, sl])
        ti = _half_tanh_half(_dot(xcb, wi_ref[nb]) + bi_ref[:, sl])
        log_a = neg_half_c_sp[:, sl] * (1.0 + tr)
        th = jnp.tanh(log_a)
        m2 = -2.0 * th / (1.0 - th)
        mult = jnp.where(m2 > 0.0, m2 * lax.rsqrt(m2), 0.0)
        if first_pos_zero:
            mult = jnp.where(jnp.logical_and(first_row, t == 0), 1.0, mult)
        a_ref[:, sl] = jnp.exp(log_a)
        b_ref[:, sl] = mult * ((0.5 * xc) * (1.0 + ti))

    if interleaved:
        def tile_scan(i, c):
            hloc, prod = c
            rows = pl.ds(pl.multiple_of(i * SUBLANES, SUBLANES), SUBLANES)
            a = a_ref[rows, :]
            hloc = a * hloc + b_ref[rows, :]
            prod = a * prod
            b_ref[rows, :] = hloc
            a_ref[rows, :] = prod
            return hloc, prod

        init = (jnp.zeros((SUBLANES, RG_WIDTH), F32), jnp.ones((SUBLANES, RG_WIDTH), F32))
        hloc, prod = lax.fori_loop(0, seg, tile_scan, init, unroll=4)
        carry = h_ref[...]
        for s in range(SUBLANES):
            cin_ref[s:s + 1, :] = carry
            carry = hloc[s:s + 1, :] + prod[s:s + 1, :] * carry
        h_ref[...] = carry
        cin = cin_ref[...]

        def tile_out(i, _):
            rows = pl.ds(pl.multiple_of(i * SUBLANES, SUBLANES), SUBLANES)
            h = b_ref[rows, :] + a_ref[rows, :] * cin
            b_ref[rows, :] = h * gate_ref[rows, :]
            return 0

        lax.fori_loop(0, seg, tile_out, 0, unroll=4)
        o_ref[0] = _dot(perm_ref[1], b_ref[...].astype(BF16)).astype(o_ref.dtype)
    else:
        def step_scan(i, h):
            h = a_ref[pl.ds(i, 1), :] * h + b_ref[pl.ds(i, 1), :]
            b_ref[pl.ds(i, 1), :] = h
            return h

        h_ref[...] = lax.fori_loop(0, tb, step_scan, h_ref[...], unroll=8)
        o_ref[0] = (b_ref[...] * gate_ref[...]).astype(o_ref.dtype)

    @pl.when(t == pl.num_programs(1) - 1)
    def _():
        h_out_ref[0] = h_ref[...]
        buf_out_ref[0] = tail_ref[...]


def _rglru(hn, w, cw, cb, wr, br, wi, bi, lam, h0, buf0, tb, pos0):
    b, t, _ = hn.shape
    nt = t // tb
    interleaved = tb % LANES == 0
    vec = lambda a: a.reshape(1, RG_WIDTH)
    return pl.pallas_call(
        functools.partial(_rg_kernel, tb=tb, first_pos_zero=(pos0 == 0), interleaved=interleaved),
        grid=(b, nt),
        in_specs=[
            pl.BlockSpec((1, tb, D_MODEL), lambda i, j: (i, j, 0)),
            _resident((D_MODEL, RG_COLS)),
            _resident((RG_CONV, RG_WIDTH)),
            _resident((1, RG_WIDTH)),
            _resident((RG_BLOCKS, RG_BLOCK, RG_BLOCK)),
            _resident((1, RG_WIDTH)),
            _resident((RG_BLOCKS, RG_BLOCK, RG_BLOCK)),
            _resident((1, RG_WIDTH)),
            _resident((1, RG_WIDTH)),
            pl.BlockSpec((1, 1, RG_WIDTH), lambda i, j: (i, 0, 0)),
            pl.BlockSpec((1, RG_CONV - 1, RG_WIDTH), lambda i, j: (i, 0, 0)),
        ],
        out_specs=[
            pl.BlockSpec((1, tb, RG_WIDTH), lambda i, j: (i, j, 0)),
            pl.BlockSpec((1, 1, RG_WIDTH), lambda i, j: (i, 0, 0)),
            pl.BlockSpec((1, RG_CONV - 1, RG_WIDTH), lambda i, j: (i, 0, 0)),
        ],
        out_shape=[
            jax.ShapeDtypeStruct((b, t, RG_WIDTH), BF16),
            jax.ShapeDtypeStruct((b, 1, RG_WIDTH), F32),
            jax.ShapeDtypeStruct((b, RG_CONV - 1, RG_WIDTH), F32),
        ],
        scratch_shapes=[
            pltpu.VMEM((tb, RG_WIDTH), F32),
            pltpu.VMEM((tb + (RG_CONV - 1) * SUBLANES, RG_WIDTH), F32),
            pltpu.VMEM((tb, RG_WIDTH), F32),
            pltpu.VMEM((tb, RG_WIDTH), F32),
            pltpu.VMEM((2, tb, tb), BF16),
            pltpu.VMEM((RG_CONV - 1, RG_WIDTH), F32),
            pltpu.VMEM((SUBLANES, RG_WIDTH), F32),
            pltpu.VMEM((1, RG_WIDTH), F32),
        ],
        compiler_params=_cparams(("parallel", "arbitrary")),
        name="rglru",
    )(hn, w, cw, vec(cb), wr, vec(br), wi, vec(bi), vec(lam), h0.reshape(b, 1, RG_WIDTH), buf0)


def _merge_kernel(hn_ref, oret_ref, ohg_ref, org_ref, x_ref, wg_ref, wb_ref, wo_ref, out_ref):
    hn = hn_ref[...]
    mixed = _sigmoid(_dot(hn, wg_ref[:, 0:D_MODEL])) * _dot(oret_ref[...], wb_ref[0:RET_V, :])
    mixed = mixed + _sigmoid(_dot(hn, wg_ref[:, D_MODEL:2 * D_MODEL])) * _dot(
        ohg_ref[...], wb_ref[RET_V:RET_V + HG_V, :])
    mixed = mixed + _sigmoid(_dot(hn, wg_ref[:, 2 * D_MODEL:3 * D_MODEL])) * _dot(
        org_ref[...], wb_ref[RET_V + HG_V:RET_V + HG_V + RG_WIDTH, :])
    out_ref[...] = x_ref[...] + _dot(mixed.astype(BF16), wo_ref[...])


def _merge(hn, o_ret, o_hg, o_rg, x, wg, wb, wo, tm):
    rows = x.shape[0]
    rowspec = lambda width: pl.BlockSpec((tm, width), lambda i: (i, 0))
    return pl.pallas_call(
        _merge_kernel,
        grid=(rows // tm,),
        in_specs=[rowspec(D_MODEL), rowspec(RET_V), rowspec(HG_V), rowspec(RG_WIDTH), rowspec(D_MODEL),
                  _resident((D_MODEL, GATE_COLS)), _resident((RET_V + HG_V + RG_WIDTH, D_MODEL)),
                  _resident((D_MODEL, D_MODEL))],
        out_specs=rowspec(D_MODEL),
        out_shape=jax.ShapeDtypeStruct((rows, D_MODEL), F32),
        compiler_params=_cparams(("parallel",)),
        name="merge",
    )(hn, o_ret, o_hg, o_rg, x, wg, wb, wo)


def _ffn_kernel(x_ref, n2_ref, wup_ref, cw_ref, cb_ref, wdn_ref, buf0_ref, fn_ref, out_ref, buf_out_ref,
                abuf_ref, *, tb, final_norm):
    t = pl.program_id(1)
    hist = SUBLANES

    @pl.when(t == 0)
    def _():
        abuf_ref[0:hist, :] = jnp.zeros((hist, D_FF), F32)
        abuf_ref[hist - (FFN_CONV - 1):hist, :] = buf0_ref[0]

    x = x_ref[0]
    hn = _rms(x, n2_ref[...]).astype(BF16)
    abuf_ref[hist:hist + tb, :] = _dot(hn, wup_ref[:, 0:D_FF])
    gate = _dot(hn, wup_ref[:, D_FF:2 * D_FF])
    a = cb_ref[...]
    for j in range(FFN_CONV):
        a = a + cw_ref[j:j + 1, :] * abuf_ref[hist - (FFN_CONV - 1) + j:hist - (FFN_CONV - 1) + j + tb, :]
    y = x + _dot((_gelu(a) * gate).astype(BF16), wdn_ref[...])
    if final_norm:
        y = _rms(y, fn_ref[...])
    out_ref[0] = y

    @pl.when(t == pl.num_programs(1) - 1)
    def _():
        buf_out_ref[0] = abuf_ref[tb + hist - (FFN_CONV - 1):tb + hist, :]

    abuf_ref[0:hist, :] = abuf_ref[tb:tb + hist, :]


def _convffn(x, n2, wup, cw, cb, wdn, buf0, fn, tb, final_norm):
    b, t, _ = x.shape
    nt = t // tb
    return pl.pallas_call(
        functools.partial(_ffn_kernel, tb=tb, final_norm=final_norm),
        grid=(b, nt),
        in_specs=[
            pl.BlockSpec((1, tb, D_MODEL), lambda i, j: (i, j, 0)),
            _resident((1, D_MODEL)),
            _resident((D_MODEL, 2 * D_FF)),
            _resident((FFN_CONV, D_FF)),
            _resident((1, D_FF)),
            _resident((D_FF, D_MODEL)),
            pl.BlockSpec((1, FFN_CONV - 1, D_FF), lambda i, j: (i, 0, 0)),
            _resident((1, D_MODEL)),
        ],
        out_specs=[
            pl.BlockSpec((1, tb, D_MODEL), lambda i, j: (i, j, 0)),
            pl.BlockSpec((1, FFN_CONV - 1, D_FF), lambda i, j: (i, 0, 0)),
        ],
        out_shape=[
            jax.ShapeDtypeStruct((b, t, D_MODEL), F32),
            jax.ShapeDtypeStruct((b, FFN_CONV - 1, D_FF), F32),
        ],
        scratch_shapes=[pltpu.VMEM((tb + SUBLANES, D_FF), F32)],
        compiler_params=_cparams(("parallel", "arbitrary")),
        name="convffn",
    )(x, n2.reshape(1, D_MODEL), wup, cw, cb.reshape(1, D_FF), wdn, buf0, fn.reshape(1, D_MODEL))


def _rope_tables(t, pos0):
    half = RET_DK // 2
    inv = jnp.power(ROPE_BASE, -jnp.arange(half, dtype=F32) / half)
    ang = (jnp.arange(t, dtype=F32) + pos0)[:, None] * inv[None, :]
    cos, sin = jnp.cos(ang), jnp.sin(ang)
    return jnp.concatenate([cos, cos], axis=-1), jnp.concatenate([-sin, sin], axis=-1)


def _block_sizes(t):
    if t <= CHUNK:
        return t, t, t
    return TIME_BLOCK, WIDE_TIME_BLOCK, CHUNK


def _trunk(x, pos0, r0, s0, h0, rgb0, ffb0, p):
    b, t, _ = x.shape
    tb, tb_wide, chunk = _block_sizes(t)
    rows = b * t
    tm = min(rows, ROW_BLOCK)
    cos2, sin2 = _rope_tables(t, pos0)
    depth = p["norm1_w"].shape[0]
    states = []
    x2d = x.reshape(rows, D_MODEL)
    for l in range(depth):
        o_ret, r_new, hn = _retention(
            x2d.reshape(b, t, D_MODEL), p["norm1_w"][l], p["w_ret"][l], cos2, sin2, r0[l], tb, chunk)
        hn2d = hn.reshape(rows, D_MODEL)
        o_hg, s_new = _hgrn(hn, p["w_hg"][l], p["hg_lb"], p["hg_norm_w"][l], s0[l], tb_wide, chunk, l)
        o_rg, h_new, rgb_new = _rglru(
            hn, p["w_rg"][l], p["rg_conv_w"][l], p["rg_conv_b"][l], p["rg_w_r"][l], p["rg_b_r"][l],
            p["rg_w_i"][l], p["rg_b_i"][l], p["rg_lambda"][l], h0[l], rgb0[l], tb, pos0)
        x2d = _merge(hn2d, o_ret.reshape(rows, RET_V), o_hg.reshape(rows, HG_V), o_rg.reshape(rows, RG_WIDTH),
                     x2d, p["w_gate"][l], p["w_branch"][l], p["w_out"][l], tm)
        x3d, ffb_new = _convffn(
            x2d.reshape(b, t, D_MODEL), p["norm2_w"][l], p["w_up"][l], p["ffn_conv_w"][l], p["ffn_conv_b"][l],
            p["w_down"][l], ffb0[l], p["final_norm_w"], tb_wide, l == depth - 1)
        x2d = x3d.reshape(rows, D_MODEL)
        states.append((r_new, s_new, h_new.reshape(b, RG_WIDTH), rgb_new, ffb_new))
    new_states = tuple(jnp.stack(st, axis=0) for st in zip(*states))
    return x2d.reshape(b, t, D_MODEL), new_states


def kernel(x_prompt, x_sample, state_ret, state_hgrn, state_rglru, cache_rg_conv, cache_ffn_conv,
           norm1_w, w_in, w_branch, w_out, rg_conv_w, rg_conv_b, rg_w_r, rg_b_r, rg_w_i, rg_b_i,
           rg_lambda, hg_lb, hg_norm_w, norm2_w, w_up, ffn_conv_w, ffn_conv_b, w_down, final_norm_w):
    depth = w_in.shape[0]
    batch = x_prompt.shape[0]
    bf = lambda a: a.astype(BF16)
    p = dict(
        norm1_w=norm1_w, norm2_w=norm2_w, final_norm_w=final_norm_w, hg_lb=hg_lb, hg_norm_w=hg_norm_w,
        w_ret=bf(w_in[:, :, 0:OFF_HG]), w_hg=bf(w_in[:, :, OFF_HG:OFF_RG]),
        w_rg=bf(w_in[:, :, OFF_RG:OFF_GATE]), w_gate=bf(w_in[:, :, OFF_GATE:OFF_GATE + GATE_COLS]),
        w_branch=bf(w_branch), w_out=bf(w_out), rg_conv_w=rg_conv_w, rg_conv_b=rg_conv_b,
        rg_w_r=bf(rg_w_r), rg_b_r=rg_b_r, rg_w_i=bf(rg_w_i), rg_b_i=rg_b_i, rg_lambda=rg_lambda,
        w_up=bf(w_up), ffn_conv_w=ffn_conv_w, ffn_conv_b=ffn_conv_b, w_down=bf(w_down),
    )
    zeros = lambda *shape: jnp.zeros((depth, batch) + shape, F32)
    y_p, (ret_p, hg_p, rgh_p, rgc_p, ffc_p) = _trunk(
        x_prompt, 0, zeros(RET_HEADS, RET_DK, RET_DV), zeros(HG_HEADS, HG_DK, HG_DV), zeros(RG_WIDTH),
        zeros(RG_CONV - 1, RG_WIDTH), zeros(FFN_CONV - 1, D_FF), p)
    y_s, (ret_s, hg_s, rgh_s, rgc_s, ffc_s) = _trunk(
        x_sample, PAST_LEN, state_ret, state_hgrn, state_rglru, cache_rg_conv, cache_ffn_conv, p)
    return (y_p, y_s, ret_p, ret_s, hg_p, hg_s, rgh_p, rgh_s, rgc_p, rgc_s, ffc_p, ffc_s)
```

```python
import functools
import math

import jax
import jax.numpy as jnp
from jax import lax
from jax.experimental import pallas as pl
from jax.experimental.pallas import tpu as pltpu

F32 = jnp.float32
BF16 = jnp.bfloat16

D_MODEL = 1024
PAST_LEN = 2048
CHUNK = 64
RET_HEADS, RET_DK, RET_DV = 4, 128, 256
RET_QK = RET_HEADS * RET_DK
RET_V = RET_HEADS * RET_DV
ROPE_BASE = 10000.0
HG_HEADS, HG_DK, HG_DV = 8, 128, 128
HG_K = HG_HEADS * HG_DK
HG_V = HG_HEADS * HG_DV
HG_MIN_F = 1e-6
HG_SAFE_DECAY = 120.0
RG_BLOCKS, RG_BLOCK = 5, 256
RG_WIDTH = RG_BLOCKS * RG_BLOCK
RG_CONV = 4
RG_C = 8.0
D_FF = 2816
FFN_CONV = 3
EPS = 1e-6

RET_COLS = 2 * RET_QK + 2 * RET_V
HG_COLS = 2 * HG_K + 2 * HG_V
RG_COLS = 2 * RG_WIDTH
GATE_COLS = 3 * D_MODEL
OFF_HG = RET_COLS
OFF_RG = OFF_HG + HG_COLS
OFF_GATE = OFF_RG + RG_COLS

VMEM_LIMIT_BYTES = 56 * 1024 * 1024
SUBLANES = 8
LANES = 128
TIME_BLOCK = 256
WIDE_TIME_BLOCK = 512
ROW_BLOCK = 512
GELU_C = math.sqrt(2.0 / math.pi)

LOG_GAMMA = tuple(math.log1p(-(2.0 ** (-5.0 - h))) for h in range(RET_HEADS))


def _cparams(sem):
    return pltpu.CompilerParams(dimension_semantics=sem, vmem_limit_bytes=VMEM_LIMIT_BYTES)


def _resident(shape):
    nd = len(shape)
    return pl.BlockSpec(shape, lambda *_: (0,) * nd, pipeline_mode=pl.Buffered(1))


def _sigmoid_pair(z):
    e = jnp.exp(-jnp.abs(z))
    d = 1.0 / (1.0 + e)
    ed = e * d
    pos = z >= 0.0
    return jnp.where(pos, d, ed), jnp.where(pos, ed, d)


def _half_tanh_half(x):
    return jnp.tanh(0.5 * x)


def _sigmoid(x):
    return 0.5 * (1.0 + _half_tanh_half(x))


def _silu(x):
    return (0.5 * x) * (1.0 + _half_tanh_half(x))


def _gelu(x):
    inner = x * (GELU_C + (GELU_C * 0.044715) * (x * x))
    return (0.5 * x) * (1.0 + jnp.tanh(inner))


def _rms(x, w):
    ms = jnp.mean(x * x, axis=-1, keepdims=True)
    return x * lax.rsqrt(ms + EPS) * w


def _dot(a, b):
    return jnp.dot(a, b, preferred_element_type=F32)


def _dot_nt(a, b):
    return lax.dot_general(a, b, (((1,), (1,)), ((), ())), preferred_element_type=F32)


def _dot_tn(a, b):
    return lax.dot_general(a, b, (((0,), (0,)), ((), ())), preferred_element_type=F32)


def _ret_kernel(x_ref, n1_ref, w_ref, cos_ref, sin_ref, r0_ref, o_ref, r_ref, hn_ref, proj_ref, dec_ref, qd_ref,
                kd_ref, *, tb, chunk):
    t = pl.program_id(1)

    @pl.when(t == 0)
    def _():
        r_ref[...] = r0_ref[...]
        row = lax.broadcasted_iota(jnp.int32, (tb, tb), 0)
        col = lax.broadcasted_iota(jnp.int32, (tb, tb), 1)
        dist = jnp.abs(row - col).astype(F32)
        chunk_bits = chunk.bit_length() - 1
        later_chunk = col >= ((row >> chunk_bits) + 1) * chunk
        n = lax.broadcasted_iota(jnp.int32, (tb, RET_DK), 0).astype(F32)
        for h in range(RET_HEADS):
            lg = LOG_GAMMA[h]
            dec_ref[h] = jnp.where(later_chunk, 0.0, jnp.exp(dist * lg))
            qd_ref[h] = jnp.exp(lg * (n + 1.0))
            kd_ref[h] = jnp.exp(lg * (float(tb) - 1.0 - n))

    hn = _rms(x_ref[0], n1_ref[...]).astype(BF16)
    hn_ref[0] = hn
    proj_ref[...] = _dot(hn, w_ref[...])
    cos = cos_ref[...]
    sin = sin_ref[...]
    for h in range(RET_HEADS):
        q = proj_ref[:, h * RET_DK:(h + 1) * RET_DK]
        k = proj_ref[:, RET_QK + h * RET_DK:RET_QK + (h + 1) * RET_DK]
        vb = proj_ref[:, 2 * RET_QK + h * RET_DV:2 * RET_QK + (h + 1) * RET_DV].astype(BF16)
        q = q * cos + pltpu.roll(q, RET_DK // 2, 1) * sin
        k = (k * cos + pltpu.roll(k, RET_DK // 2, 1) * sin) * (RET_DK ** -0.5)
        scores = _dot_nt(q.astype(BF16), k.astype(BF16)) * dec_ref[h]
        r = r_ref[0, h]
        o = _dot(scores.astype(BF16), vb) + _dot((q * qd_ref[h]).astype(BF16), r.astype(BF16))
        r_ref[0, h] = math.exp(LOG_GAMMA[h] * tb) * r + _dot_tn((k * kd_ref[h]).astype(BF16), vb)
        o = o * lax.rsqrt(jnp.mean(o * o, axis=-1, keepdims=True) + EPS)
        g = proj_ref[:, 2 * RET_QK + RET_V + h * RET_DV:2 * RET_QK + RET_V + (h + 1) * RET_DV]
        o_ref[0, :, h * RET_DV:(h + 1) * RET_DV] = (o * _silu(g)).astype(o_ref.dtype)


def _retention(x, n1, w, cos2, sin2, r0, tb, chunk):
    b, t, _ = x.shape
    nt = t // tb
    return pl.pallas_call(
        functools.partial(_ret_kernel, tb=tb, chunk=chunk),
        grid=(b, nt),
        in_specs=[
            pl.BlockSpec((1, tb, D_MODEL), lambda i, j: (i, j, 0)),
            _resident((1, D_MODEL)),
            _resident((D_MODEL, RET_COLS)),
            pl.BlockSpec((tb, RET_DK), lambda i, j: (j, 0)),
            pl.BlockSpec((tb, RET_DK), lambda i, j: (j, 0)),
            pl.BlockSpec((1, RET_HEADS, RET_DK, RET_DV), lambda i, j: (i, 0, 0, 0)),
        ],
        out_specs=[
            pl.BlockSpec((1, tb, RET_V), lambda i, j: (i, j, 0)),
            pl.BlockSpec((1, RET_HEADS, RET_DK, RET_DV), lambda i, j: (i, 0, 0, 0)),
            pl.BlockSpec((1, tb, D_MODEL), lambda i, j: (i, j, 0)),
        ],
        out_shape=[
            jax.ShapeDtypeStruct((b, t, RET_V), BF16),
            jax.ShapeDtypeStruct((b, RET_HEADS, RET_DK, RET_DV), F32),
            jax.ShapeDtypeStruct((b, t, D_MODEL), BF16),
        ],
        scratch_shapes=[
            pltpu.VMEM((tb, RET_COLS), F32),
            pltpu.VMEM((RET_HEADS, tb, tb), F32),
            pltpu.VMEM((RET_HEADS, tb, RET_DK), F32),
            pltpu.VMEM((RET_HEADS, tb, RET_DK), F32),
        ],
        compiler_params=_cparams(("parallel", "arbitrary")),
        name="retention",
    )(x, n1.reshape(1, D_MODEL), w, cos2, sin2, r0)


def _hg_kernel(hn_ref, w_ref, lb_ref, nw_ref, s0_ref, o_ref, s_ref, proj_ref, st2_ref, kk_ref, bc_ref, qs_ref,
               acc_ref, *, chunk, n_chunks, layer, depth):
    t = pl.program_id(1)
    st_old = st2_ref.at[jnp.bitwise_and(t, 1)]
    st_new = st2_ref.at[jnp.bitwise_and(t + 1, 1)]

    @pl.when(t == 0)
    def _():
        for h in range(HG_HEADS):
            st2_ref[0, h] = s0_ref[0, h].T

    def project(lo, hi):
        proj_ref[:, lo:hi] = _dot(hn_ref[0], w_ref[:, lo:hi])

    project(HG_K, 2 * HG_K)
    project(0, HG_K)

    lrows = [lb_ref[d:d + 1, :] for d in range(depth)]
    mx = functools.reduce(jnp.maximum, lrows)
    ex = [jnp.exp(r - mx) for r in lrows]
    den = functools.reduce(lambda a, c: a + c, ex)
    lb = jnp.zeros_like(mx)
    for d in range(1, layer + 1):
        lb = lb + ex[d] / den
    one_m_lb = 1.0 - lb

    row = lax.broadcasted_iota(jnp.int32, (chunk, chunk), 0)
    col = lax.broadcasted_iota(jnp.int32, (chunk, chunk), 1)
    causal = row >= col
    tri = jnp.where(causal, 1.0, 0.0).astype(BF16)
    nw = nw_ref[...]

    def gates(c):
        sg, sgn = _sigmoid_pair(proj_ref[c * chunk:(c + 1) * chunk, HG_K:2 * HG_K])
        if layer == 0:
            fg, kk = sg, sgn
        else:
            fg, kk = lb + one_m_lb * sg, one_m_lb * sgn
        logf = jnp.log(jnp.maximum(fg, HG_MIN_F))
        g_hi = logf.astype(BF16)
        g_lo = (logf - g_hi.astype(F32)).astype(BF16)
        return kk, _dot(tri, g_hi) + _dot(tri, g_lo)

    steepest = []

    def prologue(c):
        rows = slice(c * chunk, (c + 1) * chunk)
        kk, bc = gates(c)
        total = bc[chunk - 1:chunk, :]
        steepest.append(total)
        mid = 0.5 * total
        bm = bc - mid
        qt = (_silu(proj_ref[rows, 0:HG_K]) * jnp.exp(bm)).astype(BF16)
        kt = (kk * jnp.exp(-bm)).astype(BF16)
        return dict(rows=rows, emid=jnp.exp(mid), qt=qt, kt=kt, src=st_old if c == 0 else st_new)

    def head_products(pro, h):
        sl = slice(h * HG_DK, (h + 1) * HG_DK)
        rows = pro["rows"]
        vb = proj_ref[rows, 2 * HG_K + h * HG_DV:2 * HG_K + (h + 1) * HG_DV].astype(BF16)
        qt, kt = pro["qt"][:, sl], pro["kt"][:, sl]
        return dict(vb=vb, qt=qt, emid=pro["emid"][:, sl], rows=rows, h=h, src=pro["src"],
                    scores=_dot_nt(qt, kt), kv=_dot_tn(vb, kt))

    def head_finish(hp):
        h, rows, emid = hp["h"], hp["rows"], hp["emid"]
        attn = jnp.where(causal, hp["scores"], 0.0)
        st_e = hp["src"][h] * emid
        o = _dot(attn.astype(BF16), hp["vb"]) + _dot_nt(hp["qt"], st_e.astype(BF16))
        st_new[h] = (st_e + hp["kv"]) * emid
        write_output(o, rows, h)

    def write_output(o, rows, h):
        o = o * lax.rsqrt(jnp.mean(o * o, axis=-1, keepdims=True) + EPS)
        gate = proj_ref[rows, 2 * HG_K + HG_V + h * HG_DV:2 * HG_K + HG_V + (h + 1) * HG_DV]
        o_ref[0, rows, h * HG_DV:(h + 1) * HG_DV] = (
            o * nw[:, h * HG_DV:(h + 1) * HG_DV] * _silu(gate)).astype(o_ref.dtype)

    pro = prologue(0)
    project(2 * HG_K, 2 * HG_K + HG_V)
    for c in range(n_chunks):
        products = [head_products(pro, h) for h in range(HG_HEADS)]
        if c == 0:
            project(2 * HG_K + HG_V, 2 * HG_K + 2 * HG_V)
        if c + 1 < n_chunks:
            pro = prologue(c + 1)
        for hp in products:
            head_finish(hp)

    steep = jnp.min(functools.reduce(jnp.minimum, steepest)) < -HG_SAFE_DECAY

    @pl.when(steep)
    def _():
        live_rows = lax.broadcasted_iota(jnp.int32, (chunk, HG_DV), 0)
        for c in range(n_chunks):
            rows = slice(c * chunk, (c + 1) * chunk)
            src = st_old if c == 0 else st_new
            kk, bc = gates(c)
            kk_ref[...] = kk
            bc_ref[...] = bc
            qs_ref[...] = _silu(proj_ref[rows, 0:HG_K])
            acc_ref[...] = jnp.zeros(acc_ref.shape, F32)

            def source_frame(s, carry):
                frame = pl.ds(c * chunk + s, 1)
                decay = jnp.exp(jnp.minimum(bc_ref[...] - bc_ref[pl.ds(s, 1), :], 0.0))
                term = qs_ref[...] * decay * kk_ref[pl.ds(s, 1), :]
                v_row = proj_ref[frame, 2 * HG_K:2 * HG_K + HG_V]
                for h in range(HG_HEADS):
                    sl = slice(h * HG_DK, (h + 1) * HG_DK)
                    weight = jnp.sum(term[:, sl], axis=1, keepdims=True)
                    acc_ref[:, sl] += jnp.where(live_rows >= s, weight * v_row[:, sl], 0.0)
                return carry

            lax.fori_loop(0, chunk, source_frame, 0)
            total = bc[chunk - 1:chunk, :]
            q_dec = (qs_ref[...] * jnp.exp(bc)).astype(BF16)
            k_dec = (kk * jnp.exp(total - bc)).astype(BF16)
            s_dec = jnp.exp(total)
            for h in range(HG_HEADS):
                sl = slice(h * HG_DK, (h + 1) * HG_DK)
                vb = proj_ref[rows, 2 * HG_K + h * HG_DV:2 * HG_K + (h + 1) * HG_DV].astype(BF16)
                st = src[h]
                o = acc_ref[:, sl] + _dot_nt(q_dec[:, sl], st.astype(BF16))
                st_new[h] = st * s_dec[:, sl] + _dot_tn(vb, k_dec[:, sl])
                write_output(o, rows, h)

    @pl.when(t == pl.num_programs(1) - 1)
    def _():
        for h in range(HG_HEADS):
            s_ref[0, h] = st_new[h].T


def _hgrn(hn, w, hg_lb, norm_w, s0, tb, chunk, layer):
    b, t, _ = hn.shape
    nt = t // tb
    depth = hg_lb.shape[0]
    return pl.pallas_call(
        functools.partial(_hg_kernel, chunk=chunk, n_chunks=tb // chunk, layer=layer, depth=depth),
        grid=(b, nt),
        in_specs=[
            pl.BlockSpec((1, tb, D_MODEL), lambda i, j: (i, j, 0)),
            _resident((D_MODEL, HG_COLS)),
            _resident((depth, HG_K)),
            _resident((1, HG_V)),
            pl.BlockSpec((1, HG_HEADS, HG_DK, HG_DV), lambda i, j: (i, 0, 0, 0)),
        ],
        out_specs=[
            pl.BlockSpec((1, tb, HG_V), lambda i, j: (i, j, 0)),
            pl.BlockSpec((1, HG_HEADS, HG_DK, HG_DV), lambda i, j: (i, 0, 0, 0)),
        ],
        out_shape=[
            jax.ShapeDtypeStruct((b, t, HG_V), BF16),
            jax.ShapeDtypeStruct((b, HG_HEADS, HG_DK, HG_DV), F32),
        ],
        scratch_shapes=[
            pltpu.VMEM((tb, HG_COLS), F32),
            pltpu.VMEM((2, HG_HEADS, HG_DV, HG_DK), F32),
            pltpu.VMEM((chunk, HG_K), F32),
            pltpu.VMEM((chunk, HG_K), F32),
            pltpu.VMEM((chunk, HG_K), F32),
            pltpu.VMEM((chunk, HG_V), F32),
        ],
        compiler_params=_cparams(("parallel", "arbitrary")),
        name="hgrn2",
    )(hn, w, hg_lb, norm_w.reshape(1, HG_V), s0)


def _rg_kernel(hn_ref, w_ref, cw_ref, cb_ref, wr_ref, br_ref, wi_ref, bi_ref, lam_ref, h0_ref, buf0_ref,
               o_ref, h_out_ref, buf_out_ref, gate_ref, ubuf_ref, a_ref, b_ref, perm_ref, tail_ref, cin_ref, h_ref,
               *, tb, first_pos_zero, interleaved):
    t = pl.program_id(1)
    seg = tb // SUBLANES
    step = SUBLANES if interleaved else 1
    hist = -(-(RG_CONV - 1) * step // SUBLANES) * SUBLANES
    n_hist = RG_CONV - 1

    @pl.when(t == 0)
    def _():
        tail_ref[...] = buf0_ref[0]
        h_ref[...] = h0_ref[0]
        if interleaved:
            row = lax.broadcasted_iota(jnp.int32, (tb, tb), 0)
            col = lax.broadcasted_iota(jnp.int32, (tb, tb), 1)
            frame_of = lambda p: jnp.bitwise_and(p, SUBLANES - 1) * seg + (p >> (SUBLANES.bit_length() - 1))
            perm_ref[0] = jnp.where(col == frame_of(row), 1.0, 0.0).astype(BF16)
            perm_ref[1] = jnp.where(row == frame_of(col), 1.0, 0.0).astype(BF16)

    hn = hn_ref[0]
    if interleaved:
        hn = _dot(perm_ref[0], hn).astype(BF16)
    gate_ref[...] = _dot(hn, w_ref[:, RG_WIDTH:2 * RG_WIDTH])
    gate_ref[...] = _gelu(gate_ref[...])

    ubuf_ref[hist:hist + tb, :] = _dot(hn, w_ref[:, 0:RG_WIDTH])
    if interleaved:
        row0 = lax.broadcasted_iota(jnp.int32, (SUBLANES, RG_WIDTH), 0) == 0
        for k in range(n_hist):
            src = hist + (seg - n_hist + k) * SUBLANES
            moved = pltpu.roll(ubuf_ref[src:src + SUBLANES, :], 1, 0)
            ubuf_ref[k * SUBLANES:(k + 1) * SUBLANES, :] = jnp.where(row0, tail_ref[k:k + 1, :], moved)
        for k in range(n_hist):
            last = hist + (seg - n_hist + k) * SUBLANES + SUBLANES - 1
            tail_ref[k:k + 1, :] = ubuf_ref[last:last + 1, :]
    else:
        ubuf_ref[hist - n_hist:hist, :] = tail_ref[...]
        tail_ref[...] = ubuf_ref[hist + tb - n_hist:hist + tb, :]

    lam = lam_ref[...]
    neg_half_c_sp = (-0.5 * RG_C) * (jnp.maximum(-lam, 0.0) + jnp.log1p(jnp.exp(-jnp.abs(lam))))
    first_row = lax.broadcasted_iota(jnp.int32, (tb, RG_BLOCK), 0) == 0

    for nb in range(RG_BLOCKS):
        sl = slice(nb * RG_BLOCK, (nb + 1) * RG_BLOCK)
        xc = cb_ref[:, sl]
        for j in range(RG_CONV):
            lo = hist - (n_hist - j) * step
            xc = xc + cw_ref[j:j + 1, sl] * ubuf_ref[lo:lo + tb, sl]
        xcb = xc.astype(BF16)
        tr = _half_tanh_half(_dot(xcb, wr_ref[nb]) + br_ref[:, sl])
        ti = _half_tanh_half(_dot(xcb, wi_ref[nb]) + bi_ref[:, sl])
        log_a = neg_half_c_sp[:, sl] * (1.0 + tr)
        th = jnp.tanh(log_a)
        m2 = -2.0 * th / (1.0 - th)
        mult = jnp.where(m2 > 0.0, m2 * lax.rsqrt(m2), 0.0)
        if first_pos_zero:
            mult = jnp.where(jnp.logical_and(first_row, t == 0), 1.0, mult)
        a_ref[:, sl] = jnp.exp(log_a)
        b_ref[:, sl] = mult * ((0.5 * xc) * (1.0 + ti))

    if interleaved:
        def tile_scan(i, c):
            hloc, prod = c
            rows = pl.ds(pl.multiple_of(i * SUBLANES, SUBLANES), SUBLANES)
            a = a_ref[rows, :]
            hloc = a * hloc + b_ref[rows, :]
            prod = a * prod
            b_ref[rows, :] = hloc
            a_ref[rows, :] = prod
            return hloc, prod

        init = (jnp.zeros((SUBLANES, RG_WIDTH), F32), jnp.ones((SUBLANES, RG_WIDTH), F32))
        hloc, prod = lax.fori_loop(0, seg, tile_scan, init, unroll=4)
        carry = h_ref[...]
        for s in range(SUBLANES):
            cin_ref[s:s + 1, :] = carry
            carry = hloc[s:s + 1, :] + prod[s:s + 1, :] * carry
        h_ref[...] = carry
        cin = cin_ref[...]

        def tile_out(i, _):
            rows = pl.ds(pl.multiple_of(i * SUBLANES, SUBLANES), SUBLANES)
            h = b_ref[rows, :] + a_ref[rows, :] * cin
            b_ref[rows, :] = h * gate_ref[rows, :]
            return 0

        lax.fori_loop(0, seg, tile_out, 0, unroll=4)
        o_ref[0] = _dot(perm_ref[1], b_ref[...].astype(BF16)).astype(o_ref.dtype)
    else:
        def step_scan(i, h):
            h = a_ref[pl.ds(i, 1), :] * h + b_ref[pl.ds(i, 1), :]
            b_ref[pl.ds(i, 1), :] = h
            return h

        h_ref[...] = lax.fori_loop(0, tb, step_scan, h_ref[...], unroll=8)
        o_ref[0] = (b_ref[...] * gate_ref[...]).astype(o_ref.dtype)

    @pl.when(t == pl.num_programs(1) - 1)
    def _():
        h_out_ref[0] = h_ref[...]
        buf_out_ref[0] = tail_ref[...]


def _rglru(hn, w, cw, cb, wr, br, wi, bi, lam, h0, buf0, tb, pos0):
    b, t, _ = hn.shape
    nt = t // tb
    interleaved = tb % LANES == 0
    vec = lambda a: a.reshape(1, RG_WIDTH)
    return pl.pallas_call(
        functools.partial(_rg_kernel, tb=tb, first_pos_zero=(pos0 == 0), interleaved=interleaved),
        grid=(b, nt),
        in_specs=[
            pl.BlockSpec((1, tb, D_MODEL), lambda i, j: (i, j, 0)),
            _resident((D_MODEL, RG_COLS)),
            _resident((RG_CONV, RG_WIDTH)),
            _resident((1, RG_WIDTH)),
            _resident((RG_BLOCKS, RG_BLOCK, RG_BLOCK)),
            _resident((1, RG_WIDTH)),
            _resident((RG_BLOCKS, RG_BLOCK, RG_BLOCK)),
            _resident((1, RG_WIDTH)),
            _resident((1, RG_WIDTH)),
            pl.BlockSpec((1, 1, RG_WIDTH), lambda i, j: (i, 0, 0)),
            pl.BlockSpec((1, RG_CONV - 1, RG_WIDTH), lambda i, j: (i, 0, 0)),
        ],
        out_specs=[
            pl.BlockSpec((1, tb, RG_WIDTH), lambda i, j: (i, j, 0)),
            pl.BlockSpec((1, 1, RG_WIDTH), lambda i, j: (i, 0, 0)),
            pl.BlockSpec((1, RG_CONV - 1, RG_WIDTH), lambda i, j: (i, 0, 0)),
        ],
        out_shape=[
            jax.ShapeDtypeStruct((b, t, RG_WIDTH), BF16),
            jax.ShapeDtypeStruct((b, 1, RG_WIDTH), F32),
            jax.ShapeDtypeStruct((b, RG_CONV - 1, RG_WIDTH), F32),
        ],
        scratch_shapes=[
            pltpu.VMEM((tb, RG_WIDTH), F32),
            pltpu.VMEM((tb + (RG_CONV - 1) * SUBLANES, RG_WIDTH), F32),
            pltpu.VMEM((tb, RG_WIDTH), F32),
            pltpu.VMEM((tb, RG_WIDTH), F32),
            pltpu.VMEM((2, tb, tb), BF16),
            pltpu.VMEM((RG_CONV - 1, RG_WIDTH), F32),
            pltpu.VMEM((SUBLANES, RG_WIDTH), F32),
            pltpu.VMEM((1, RG_WIDTH), F32),
        ],
        compiler_params=_cparams(("parallel", "arbitrary")),
        name="rglru",
    )(hn, w, cw, vec(cb), wr, vec(br), wi, vec(bi), vec(lam), h0.reshape(b, 1, RG_WIDTH), buf0)


def _merge_kernel(hn_ref, oret_ref, ohg_ref, org_ref, x_ref, wg_ref, wb_ref, wo_ref, out_ref):
    hn = hn_ref[...]
    mixed = _sigmoid(_dot(hn, wg_ref[:, 0:D_MODEL])) * _dot(oret_ref[...], wb_ref[0:RET_V, :])
    mixed = mixed + _sigmoid(_dot(hn, wg_ref[:, D_MODEL:2 * D_MODEL])) * _dot(
        ohg_ref[...], wb_ref[RET_V:RET_V + HG_V, :])
    mixed = mixed + _sigmoid(_dot(hn, wg_ref[:, 2 * D_MODEL:3 * D_MODEL])) * _dot(
        org_ref[...], wb_ref[RET_V + HG_V:RET_V + HG_V + RG_WIDTH, :])
    out_ref[...] = x_ref[...] + _dot(mixed.astype(BF16), wo_ref[...])


def _merge(hn, o_ret, o_hg, o_rg, x, wg, wb, wo, tm):
    rows = x.shape[0]
    rowspec = lambda width: pl.BlockSpec((tm, width), lambda i: (i, 0))
    return pl.pallas_call(
        _merge_kernel,
        grid=(rows // tm,),
        in_specs=[rowspec(D_MODEL), rowspec(RET_V), rowspec(HG_V), rowspec(RG_WIDTH), rowspec(D_MODEL),
                  _resident((D_MODEL, GATE_COLS)), _resident((RET_V + HG_V + RG_WIDTH, D_MODEL)),
                  _resident((D_MODEL, D_MODEL))],
        out_specs=rowspec(D_MODEL),
        out_shape=jax.ShapeDtypeStruct((rows, D_MODEL), F32),
        compiler_params=_cparams(("parallel",)),
        name="merge",
    )(hn, o_ret, o_hg, o_rg, x, wg, wb, wo)


def _ffn_kernel(x_ref, n2_ref, wup_ref, cw_ref, cb_ref, wdn_ref, buf0_ref, fn_ref, out_ref, buf_out_ref,
                abuf_ref, *, tb, final_norm):
    t = pl.program_id(1)
    hist = SUBLANES

    @pl.when(t == 0)
    def _():
        abuf_ref[0:hist, :] = jnp.zeros((hist, D_FF), F32)
        abuf_ref[hist - (FFN_CONV - 1):hist, :] = buf0_ref[0]

    x = x_ref[0]
    hn = _rms(x, n2_ref[...]).astype(BF16)
    abuf_ref[hist:hist + tb, :] = _dot(hn, wup_ref[:, 0:D_FF])
    gate = _dot(hn, wup_ref[:, D_FF:2 * D_FF])
    a = cb_ref[...]
    for j in range(FFN_CONV):
        a = a + cw_ref[j:j + 1, :] * abuf_ref[hist - (FFN_CONV - 1) + j:hist - (FFN_CONV - 1) + j + tb, :]
    y = x + _dot((_gelu(a) * gate).astype(BF16), wdn_ref[...])
    if final_norm:
        y = _rms(y, fn_ref[...])
    out_ref[0] = y

    @pl.when(t == pl.num_programs(1) - 1)
    def _():
        buf_out_ref[0] = abuf_ref[tb + hist - (FFN_CONV - 1):tb + hist, :]

    abuf_ref[0:hist, :] = abuf_ref[tb:tb + hist, :]


def _convffn(x, n2, wup, cw, cb, wdn, buf0, fn, tb, final_norm):
    b, t, _ = x.shape
    nt = t // tb
    return pl.pallas_call(
        functools.partial(_ffn_kernel, tb=tb, final_norm=final_norm),
        grid=(b, nt),
        in_specs=[
            pl.BlockSpec((1, tb, D_MODEL), lambda i, j: (i, j, 0)),
            _resident((1, D_MODEL)),
            _resident((D_MODEL, 2 * D_FF)),
            _resident((FFN_CONV, D_FF)),
            _resident((1, D_FF)),
            _resident((D_FF, D_MODEL)),
            pl.BlockSpec((1, FFN_CONV - 1, D_FF), lambda i, j: (i, 0, 0)),
            _resident((1, D_MODEL)),
        ],
        out_specs=[
            pl.BlockSpec((1, tb, D_MODEL), lambda i, j: (i, j, 0)),
            pl.BlockSpec((1, FFN_CONV - 1, D_FF), lambda i, j: (i, 0, 0)),
        ],
        out_shape=[
            jax.ShapeDtypeStruct((b, t, D_MODEL), F32),
            jax.ShapeDtypeStruct((b, FFN_CONV - 1, D_FF), F32),
        ],
        scratch_shapes=[pltpu.VMEM((tb + SUBLANES, D_FF), F32)],
        compiler_params=_cparams(("parallel", "arbitrary")),
        name="convffn",
    )(x, n2.reshape(1, D_MODEL), wup, cw, cb.reshape(1, D_FF), wdn, buf0, fn.reshape(1, D_MODEL))


def _rope_tables(t, pos0):
    half = RET_DK // 2
    inv = jnp.power(ROPE_BASE, -jnp.arange(half, dtype=F32) / half)
    ang = (jnp.arange(t, dtype=F32) + pos0)[:, None] * inv[None, :]
    cos, sin = jnp.cos(ang), jnp.sin(ang)
    return jnp.concatenate([cos, cos], axis=-1), jnp.concatenate([-sin, sin], axis=-1)


def _block_sizes(t):
    if t <= CHUNK:
        return t, t, t
    return TIME_BLOCK, WIDE_TIME_BLOCK, CHUNK


def _trunk(x, pos0, r0, s0, h0, rgb0, ffb0, p):
    b, t, _ = x.shape
    tb, tb_wide, chunk = _block_sizes(t)
    rows = b * t
    tm = min(rows, ROW_BLOCK)
    cos2, sin2 = _rope_tables(t, pos0)
    depth = p["norm1_w"].shape[0]
    states = []
    x2d = x.reshape(rows, D_MODEL)
    for l in range(depth):
        o_ret, r_new, hn = _retention(
            x2d.reshape(b, t, D_MODEL), p["norm1_w"][l], p["w_ret"][l], cos2, sin2, r0[l], tb, chunk)
        hn2d = hn.reshape(rows, D_MODEL)
        o_hg, s_new = _hgrn(hn, p["w_hg"][l], p["hg_lb"], p["hg_norm_w"][l], s0[l], tb_wide, chunk, l)
        o_rg, h_new, rgb_new = _rglru(
            hn, p["w_rg"][l], p["rg_conv_w"][l], p["rg_conv_b"][l], p["rg_w_r"][l], p["rg_b_r"][l],
            p["rg_w_i"][l], p["rg_b_i"][l], p["rg_lambda"][l], h0[l], rgb0[l], tb, pos0)
        x2d = _merge(hn2d, o_ret.reshape(rows, RET_V), o_hg.reshape(rows, HG_V), o_rg.reshape(rows, RG_WIDTH),
                     x2d, p["w_gate"][l], p["w_branch"][l], p["w_out"][l], tm)
        x3d, ffb_new = _convffn(
            x2d.reshape(b, t, D_MODEL), p["norm2_w"][l], p["w_up"][l], p["ffn_conv_w"][l], p["ffn_conv_b"][l],
            p["w_down"][l], ffb0[l], p["final_norm_w"], tb_wide, l == depth - 1)
        x2d = x3d.reshape(rows, D_MODEL)
        states.append((r_new, s_new, h_new.reshape(b, RG_WIDTH), rgb_new, ffb_new))
    new_states = tuple(jnp.stack(st, axis=0) for st in zip(*states))
    return x2d.reshape(b, t, D_MODEL), new_states


def kernel(x_prompt, x_sample, state_ret, state_hgrn, state_rglru, cache_rg_conv, cache_ffn_conv,
           norm1_w, w_in, w_branch, w_out, rg_conv_w, rg_conv_b, rg_w_r, rg_b_r, rg_w_i, rg_b_i,
           rg_lambda, hg_lb, hg_norm_w, norm2_w, w_up, ffn_conv_w, ffn_conv_b, w_down, final_norm_w):
    depth = w_in.shape[0]
    batch = x_prompt.shape[0]
    bf = lambda a: a.astype(BF16)
    p = dict(
        norm1_w=norm1_w, norm2_w=norm2_w, final_norm_w=final_norm_w, hg_lb=hg_lb, hg_norm_w=hg_norm_w,
        w_ret=bf(w_in[:, :, 0:OFF_HG]), w_hg=bf(w_in[:, :, OFF_HG:OFF_RG]),
        w_rg=bf(w_in[:, :, OFF_RG:OFF_GATE]), w_gate=bf(w_in[:, :, OFF_GATE:OFF_GATE + GATE_COLS]),
        w_branch=bf(w_branch), w_out=bf(w_out), rg_conv_w=rg_conv_w, rg_conv_b=rg_conv_b,
        rg_w_r=bf(rg_w_r), rg_b_r=rg_b_r, rg_w_i=bf(rg_w_i), rg_b_i=rg_b_i, rg_lambda=rg_lambda,
        w_up=bf(w_up), ffn_conv_w=ffn_conv_w, ffn_conv_b=ffn_conv_b, w_down=bf(w_down),
    )
    zeros = lambda *shape: jnp.zeros((depth, batch) + shape, F32)
    y_p, (ret_p, hg_p, rgh_p, rgc_p, ffc_p) = _trunk(
        x_prompt, 0, zeros(RET_HEADS, RET_DK, RET_DV), zeros(HG_HEADS, HG_DK, HG_DV), zeros(RG_WIDTH),
        zeros(RG_CONV - 1, RG_WIDTH), zeros(FFN_CONV - 1, D_FF), p)
    y_s, (ret_s, hg_s, rgh_s, rgc_s, ffc_s) = _trunk(
        x_sample, PAST_LEN, state_ret, state_hgrn, state_rglru, cache_rg_conv, cache_ffn_conv, p)
    return (y_p, y_s, ret_p, ret_s, hg_p, hg_s, rgh_p, rgh_s, rgc_p, rgc_s, ffc_p, ffc_s)
```

```python
import functools
import math

import jax
import jax.numpy as jnp
from jax import lax
from jax.experimental import pallas as pl
from jax.experimental.pallas import tpu as pltpu

F32 = jnp.float32
BF16 = jnp.bfloat16

D_MODEL = 1024
PAST_LEN = 2048
CHUNK = 64
RET_HEADS, RET_DK, RET_DV = 4, 128, 256
RET_QK = RET_HEADS * RET_DK
RET_V = RET_HEADS * RET_DV
ROPE_BASE = 10000.0
HG_HEADS, HG_DK, HG_DV = 8, 128, 128
HG_K = HG_HEADS * HG_DK
HG_V = HG_HEADS * HG_DV
HG_MIN_F = 1e-6
HG_GROUP_HEADS = 4
HG_SAFE_DECAY = 120.0
RG_BLOCKS, RG_BLOCK = 5, 256
RG_WIDTH = RG_BLOCKS * RG_BLOCK
RG_CONV = 4
RG_C = 8.0
D_FF = 2816
FFN_CONV = 3
EPS = 1e-6

RET_COLS = 2 * RET_QK + 2 * RET_V
HG_COLS = 2 * HG_K + 2 * HG_V
RG_COLS = 2 * RG_WIDTH
GATE_COLS = 3 * D_MODEL
OFF_HG = RET_COLS
OFF_RG = OFF_HG + HG_COLS
OFF_GATE = OFF_RG + RG_COLS

VMEM_LIMIT_BYTES = 56 * 1024 * 1024
SUBLANES = 8
LANES = 128
TIME_BLOCK = 256
WIDE_TIME_BLOCK = 512
ROW_BLOCK = 512
GELU_C = math.sqrt(2.0 / math.pi)

LOG_GAMMA = tuple(math.log1p(-(2.0 ** (-5.0 - h))) for h in range(RET_HEADS))


def _cparams(sem):
    return pltpu.CompilerParams(dimension_semantics=sem, vmem_limit_bytes=VMEM_LIMIT_BYTES)


def _resident(shape):
    nd = len(shape)
    return pl.BlockSpec(shape, lambda *_: (0,) * nd, pipeline_mode=pl.Buffered(1))


def _sigmoid_pair(z):
    e = jnp.exp(-jnp.abs(z))
    d = 1.0 / (1.0 + e)
    ed = e * d
    pos = z >= 0.0
    return jnp.where(pos, d, ed), jnp.where(pos, ed, d)


def _half_tanh_half(x):
    return jnp.tanh(0.5 * x)


def _sigmoid(x):
    return 0.5 * (1.0 + _half_tanh_half(x))


def _silu(x):
    return (0.5 * x) * (1.0 + _half_tanh_half(x))


def _gelu(x):
    inner = x * (GELU_C + (GELU_C * 0.044715) * (x * x))
    return (0.5 * x) * (1.0 + jnp.tanh(inner))


def _rms(x, w):
    ms = jnp.mean(x * x, axis=-1, keepdims=True)
    return x * lax.rsqrt(ms + EPS) * w


def _dot(a, b):
    return jnp.dot(a, b, preferred_element_type=F32)


def _dot_nt(a, b):
    return lax.dot_general(a, b, (((1,), (1,)), ((), ())), preferred_element_type=F32)


def _dot_tn(a, b):
    return lax.dot_general(a, b, (((0,), (0,)), ((), ())), preferred_element_type=F32)


def _ret_kernel(x_ref, n1_ref, w_ref, cos_ref, sin_ref, r0_ref, o_ref, r_ref, hn_ref, proj_ref, dec_ref, qd_ref,
                kd_ref, *, tb, chunk):
    t = pl.program_id(1)

    @pl.when(t == 0)
    def _():
        r_ref[...] = r0_ref[...]
        row = lax.broadcasted_iota(jnp.int32, (tb, tb), 0)
        col = lax.broadcasted_iota(jnp.int32, (tb, tb), 1)
        dist = jnp.abs(row - col).astype(F32)
        chunk_bits = chunk.bit_length() - 1
        later_chunk = col >= ((row >> chunk_bits) + 1) * chunk
        n = lax.broadcasted_iota(jnp.int32, (tb, RET_DK), 0).astype(F32)
        for h in range(RET_HEADS):
            lg = LOG_GAMMA[h]
            dec_ref[h] = jnp.where(later_chunk, 0.0, jnp.exp(dist * lg))
            qd_ref[h] = jnp.exp(lg * (n + 1.0))
            kd_ref[h] = jnp.exp(lg * (float(tb) - 1.0 - n))

    hn = _rms(x_ref[0], n1_ref[...]).astype(BF16)
    hn_ref[0] = hn
    proj_ref[...] = _dot(hn, w_ref[...])
    cos = cos_ref[...]
    sin = sin_ref[...]
    for h in range(RET_HEADS):
        q = proj_ref[:, h * RET_DK:(h + 1) * RET_DK]
        k = proj_ref[:, RET_QK + h * RET_DK:RET_QK + (h + 1) * RET_DK]
        vb = proj_ref[:, 2 * RET_QK + h * RET_DV:2 * RET_QK + (h + 1) * RET_DV].astype(BF16)
        q = q * cos + pltpu.roll(q, RET_DK // 2, 1) * sin
        k = (k * cos + pltpu.roll(k, RET_DK // 2, 1) * sin) * (RET_DK ** -0.5)
        scores = _dot_nt(q.astype(BF16), k.astype(BF16)) * dec_ref[h]
        r = r_ref[0, h]
        o = _dot(scores.astype(BF16), vb) + _dot((q * qd_ref[h]).astype(BF16), r.astype(BF16))
        r_ref[0, h] = math.exp(LOG_GAMMA[h] * tb) * r + _dot_tn((k * kd_ref[h]).astype(BF16), vb)
        o = o * lax.rsqrt(jnp.mean(o * o, axis=-1, keepdims=True) + EPS)
        g = proj_ref[:, 2 * RET_QK + RET_V + h * RET_DV:2 * RET_QK + RET_V + (h + 1) * RET_DV]
        o_ref[0, :, h * RET_DV:(h + 1) * RET_DV] = (o * _silu(g)).astype(o_ref.dtype)


def _retention(x, n1, w, cos2, sin2, r0, tb, chunk):
    b, t, _ = x.shape
    nt = t // tb
    return pl.pallas_call(
        functools.partial(_ret_kernel, tb=tb, chunk=chunk),
        grid=(b, nt),
        in_specs=[
            pl.BlockSpec((1, tb, D_MODEL), lambda i, j: (i, j, 0)),
            _resident((1, D_MODEL)),
            _resident((D_MODEL, RET_COLS)),
            pl.BlockSpec((tb, RET_DK), lambda i, j: (j, 0)),
            pl.BlockSpec((tb, RET_DK), lambda i, j: (j, 0)),
            pl.BlockSpec((1, RET_HEADS, RET_DK, RET_DV), lambda i, j: (i, 0, 0, 0)),
        ],
        out_specs=[
            pl.BlockSpec((1, tb, RET_V), lambda i, j: (i, j, 0)),
            pl.BlockSpec((1, RET_HEADS, RET_DK, RET_DV), lambda i, j: (i, 0, 0, 0)),
            pl.BlockSpec((1, tb, D_MODEL), lambda i, j: (i, j, 0)),
        ],
        out_shape=[
            jax.ShapeDtypeStruct((b, t, RET_V), BF16),
            jax.ShapeDtypeStruct((b, RET_HEADS, RET_DK, RET_DV), F32),
            jax.ShapeDtypeStruct((b, t, D_MODEL), BF16),
        ],
        scratch_shapes=[
            pltpu.VMEM((tb, RET_COLS), F32),
            pltpu.VMEM((RET_HEADS, tb, tb), F32),
            pltpu.VMEM((RET_HEADS, tb, RET_DK), F32),
            pltpu.VMEM((RET_HEADS, tb, RET_DK), F32),
        ],
        compiler_params=_cparams(("parallel", "arbitrary")),
        name="retention",
    )(x, n1.reshape(1, D_MODEL), w, cos2, sin2, r0)


def _hg_kernel(hn_ref, w_ref, lb_ref, nw_ref, s0_ref, o_ref, s_ref, proj_ref, st2_ref, kk_ref, bc_ref, qs_ref,
               acc_ref, *, chunk, n_chunks, layer, depth):
    t = pl.program_id(1)
    st_old = st2_ref.at[jnp.bitwise_and(t, 1)]
    st_new = st2_ref.at[jnp.bitwise_and(t + 1, 1)]

    @pl.when(t == 0)
    def _():
        for h in range(HG_HEADS):
            st2_ref[0, h] = s0_ref[0, h].T

    def project(lo, hi):
        proj_ref[:, lo:hi] = _dot(hn_ref[0], w_ref[:, lo:hi])

    project(HG_K, 2 * HG_K)
    project(0, HG_K)

    lrows = [lb_ref[d:d + 1, :] for d in range(depth)]
    mx = functools.reduce(jnp.maximum, lrows)
    ex = [jnp.exp(r - mx) for r in lrows]
    den = functools.reduce(lambda a, c: a + c, ex)
    lb = jnp.zeros_like(mx)
    for d in range(1, layer + 1):
        lb = lb + ex[d] / den
    one_m_lb = 1.0 - lb

    row = lax.broadcasted_iota(jnp.int32, (chunk, chunk), 0)
    col = lax.broadcasted_iota(jnp.int32, (chunk, chunk), 1)
    causal = row >= col
    tri = jnp.where(causal, 1.0, 0.0).astype(BF16)
    nw = nw_ref[...]

    def gates(c, lo=0, hi=HG_K):
        sg, sgn = _sigmoid_pair(proj_ref[c * chunk:(c + 1) * chunk, HG_K + lo:HG_K + hi])
        if layer == 0:
            fg, kk = sg, sgn
        else:
            fg, kk = lb[:, lo:hi] + one_m_lb[:, lo:hi] * sg, one_m_lb[:, lo:hi] * sgn
        logf = jnp.log(jnp.maximum(fg, HG_MIN_F))
        g_hi = logf.astype(BF16)
        g_lo = (logf - g_hi.astype(F32)).astype(BF16)
        return kk, _dot(tri, g_hi) + _dot(tri, g_lo)

    steepest = []

    group_width = HG_GROUP_HEADS * HG_DK
    units = [(c, g) for c in range(n_chunks) for g in range(HG_HEADS // HG_GROUP_HEADS)]

    def prologue(unit):
        c, g = unit
        lo, hi = g * group_width, (g + 1) * group_width
        rows = slice(c * chunk, (c + 1) * chunk)
        kk, bc = gates(c, lo, hi)
        total = bc[chunk - 1:chunk, :]
        steepest.append(total)
        mid = 0.5 * total
        bm = bc - mid
        qt = (_silu(proj_ref[rows, lo:hi]) * jnp.exp(bm)).astype(BF16)
        kt = (kk * jnp.exp(-bm)).astype(BF16)
        return dict(rows=rows, emid=jnp.exp(mid), qt=qt, kt=kt, src=st_old if c == 0 else st_new, lo=lo,
                    heads=range(g * HG_GROUP_HEADS, (g + 1) * HG_GROUP_HEADS))

    def head_products(pro, h):
        sl = slice(h * HG_DK - pro["lo"], (h + 1) * HG_DK - pro["lo"])
        rows = pro["rows"]
        vb = proj_ref[rows, 2 * HG_K + h * HG_DV:2 * HG_K + (h + 1) * HG_DV].astype(BF16)
        qt, kt = pro["qt"][:, sl], pro["kt"][:, sl]
        return dict(vb=vb, qt=qt, emid=pro["emid"][:, sl], rows=rows, h=h, src=pro["src"],
                    scores=_dot_nt(qt, kt), kv=_dot_tn(vb, kt))

    def head_finish(hp):
        h, rows, emid = hp["h"], hp["rows"], hp["emid"]
        attn = jnp.where(causal, hp["scores"], 0.0)
        st_e = hp["src"][h] * emid
        o = _dot(attn.astype(BF16), hp["vb"]) + _dot_nt(hp["qt"], st_e.astype(BF16))
        st_new[h] = (st_e + hp["kv"]) * emid
        write_output(o, rows, h)

    def write_output(o, rows, h):
        o = o * lax.rsqrt(jnp.mean(o * o, axis=-1, keepdims=True) + EPS)
        gate = proj_ref[rows, 2 * HG_K + HG_V + h * HG_DV:2 * HG_K + HG_V + (h + 1) * HG_DV]
        o_ref[0, rows, h * HG_DV:(h + 1) * HG_DV] = (
            o * nw[:, h * HG_DV:(h + 1) * HG_DV] * _silu(gate)).astype(o_ref.dtype)

    pro = prologue(units[0])
    project(2 * HG_K, 2 * HG_K + HG_V)
    for i in range(len(units)):
        products = [head_products(pro, h) for h in pro["heads"]]
        if i == 0:
            project(2 * HG_K + HG_V, 2 * HG_K + 2 * HG_V)
        if i + 1 < len(units):
            pro = prologue(units[i + 1])
        for hp in products:
            head_finish(hp)

    steep = jnp.min(functools.reduce(jnp.minimum, steepest)) < -HG_SAFE_DECAY

    @pl.when(steep)
    def _():
        live_rows = lax.broadcasted_iota(jnp.int32, (chunk, HG_DV), 0)
        for c in range(n_chunks):
            rows = slice(c * chunk, (c + 1) * chunk)
            src = st_old if c == 0 else st_new
            kk, bc = gates(c)
            kk_ref[...] = kk
            bc_ref[...] = bc
            qs_ref[...] = _silu(proj_ref[rows, 0:HG_K])
            acc_ref[...] = jnp.zeros(acc_ref.shape, F32)

            def source_frame(s, carry):
                frame = pl.ds(c * chunk + s, 1)
                decay = jnp.exp(jnp.minimum(bc_ref[...] - bc_ref[pl.ds(s, 1), :], 0.0))
                term = qs_ref[...] * decay * kk_ref[pl.ds(s, 1), :]
                v_row = proj_ref[frame, 2 * HG_K:2 * HG_K + HG_V]
                for h in range(HG_HEADS):
                    sl = slice(h * HG_DK, (h + 1) * HG_DK)
                    weight = jnp.sum(term[:, sl], axis=1, keepdims=True)
                    acc_ref[:, sl] += jnp.where(live_rows >= s, weight * v_row[:, sl], 0.0)
                return carry

            lax.fori_loop(0, chunk, source_frame, 0)
            total = bc[chunk - 1:chunk, :]
            q_dec = (qs_ref[...] * jnp.exp(bc)).astype(BF16)
            k_dec = (kk * jnp.exp(total - bc)).astype(BF16)
            s_dec = jnp.exp(total)
            for h in range(HG_HEADS):
                sl = slice(h * HG_DK, (h + 1) * HG_DK)
                vb = proj_ref[rows, 2 * HG_K + h * HG_DV:2 * HG_K + (h + 1) * HG_DV].astype(BF16)
                st = src[h]
                o = acc_ref[:, sl] + _dot_nt(q_dec[:, sl], st.astype(BF16))
                st_new[h] = st * s_dec[:, sl] + _dot_tn(vb, k_dec[:, sl])
                write_output(o, rows, h)

    @pl.when(t == pl.num_programs(1) - 1)
    def _():
        for h in range(HG_HEADS):
            s_ref[0, h] = st_new[h].T


def _hgrn(hn, w, hg_lb, norm_w, s0, tb, chunk, layer):
    b, t, _ = hn.shape
    nt = t // tb
    depth = hg_lb.shape[0]
    return pl.pallas_call(
        functools.partial(_hg_kernel, chunk=chunk, n_chunks=tb // chunk, layer=layer, depth=depth),
        grid=(b, nt),
        in_specs=[
            pl.BlockSpec((1, tb, D_MODEL), lambda i, j: (i, j, 0)),
            _resident((D_MODEL, HG_COLS)),
            _resident((depth, HG_K)),
            _resident((1, HG_V)),
            pl.BlockSpec((1, HG_HEADS, HG_DK, HG_DV), lambda i, j: (i, 0, 0, 0)),
        ],
        out_specs=[
            pl.BlockSpec((1, tb, HG_V), lambda i, j: (i, j, 0)),
            pl.BlockSpec((1, HG_HEADS, HG_DK, HG_DV), lambda i, j: (i, 0, 0, 0)),
        ],
        out_shape=[
            jax.ShapeDtypeStruct((b, t, HG_V), BF16),
            jax.ShapeDtypeStruct((b, HG_HEADS, HG_DK, HG_DV), F32),
        ],
        scratch_shapes=[
            pltpu.VMEM((tb, HG_COLS), F32),
            pltpu.VMEM((2, HG_HEADS, HG_DV, HG_DK), F32),
            pltpu.VMEM((chunk, HG_K), F32),
            pltpu.VMEM((chunk, HG_K), F32),
            pltpu.VMEM((chunk, HG_K), F32),
            pltpu.VMEM((chunk, HG_V), F32),
        ],
        compiler_params=_cparams(("parallel", "arbitrary")),
        name="hgrn2",
    )(hn, w, hg_lb, norm_w.reshape(1, HG_V), s0)


def _rg_kernel(hn_ref, w_ref, cw_ref, cb_ref, wr_ref, br_ref, wi_ref, bi_ref, lam_ref, h0_ref, buf0_ref,
               o_ref, h_out_ref, buf_out_ref, gate_ref, ubuf_ref, a_ref, b_ref, perm_ref, tail_ref, cin_ref, h_ref,
               *, tb, first_pos_zero, interleaved):
    t = pl.program_id(1)
    seg = tb // SUBLANES
    step = SUBLANES if interleaved else 1
    hist = -(-(RG_CONV - 1) * step // SUBLANES) * SUBLANES
    n_hist = RG_CONV - 1

    @pl.when(t == 0)
    def _():
        tail_ref[...] = buf0_ref[0]
        h_ref[...] = h0_ref[0]
        if interleaved:
            row = lax.broadcasted_iota(jnp.int32, (tb, tb), 0)
            col = lax.broadcasted_iota(jnp.int32, (tb, tb), 1)
            frame_of = lambda p: jnp.bitwise_and(p, SUBLANES - 1) * seg + (p >> (SUBLANES.bit_length() - 1))
            perm_ref[0] = jnp.where(col == frame_of(row), 1.0, 0.0).astype(BF16)
            perm_ref[1] = jnp.where(row == frame_of(col), 1.0, 0.0).astype(BF16)

    hn = hn_ref[0]
    if interleaved:
        hn = _dot(perm_ref[0], hn).astype(BF16)
    gate_ref[...] = _dot(hn, w_ref[:, RG_WIDTH:2 * RG_WIDTH])
    gate_ref[...] = _gelu(gate_ref[...])

    ubuf_ref[hist:hist + tb, :] = _dot(hn, w_ref[:, 0:RG_WIDTH])
    if interleaved:
        row0 = lax.broadcasted_iota(jnp.int32, (SUBLANES, RG_WIDTH), 0) == 0
        for k in range(n_hist):
            src = hist + (seg - n_hist + k) * SUBLANES
            moved = pltpu.roll(ubuf_ref[src:src + SUBLANES, :], 1, 0)
            ubuf_ref[k * SUBLANES:(k + 1) * SUBLANES, :] = jnp.where(row0, tail_ref[k:k + 1, :], moved)
        for k in range(n_hist):
            last = hist + (seg - n_hist + k) * SUBLANES + SUBLANES - 1
            tail_ref[k:k + 1, :] = ubuf_ref[last:last + 1, :]
    else:
        ubuf_ref[hist - n_hist:hist, :] = tail_ref[...]
        tail_ref[...] = ubuf_ref[hist + tb - n_hist:hist + tb, :]

    lam = lam_ref[...]
    neg_half_c_sp = (-0.5 * RG_C) * (jnp.maximum(-lam, 0.0) + jnp.log1p(jnp.exp(-jnp.abs(lam))))
    first_row = lax.broadcasted_iota(jnp.int32, (tb, RG_BLOCK), 0) == 0

    for nb in range(RG_BLOCKS):
        sl = slice(nb * RG_BLOCK, (nb + 1) * RG_BLOCK)
        xc = cb_ref[:, sl]
        for j in range(RG_CONV):
            lo = hist - (n_hist - j) * step
            xc = xc + cw_ref[j:j + 1, sl] * ubuf_ref[lo:lo + tb, sl]
        xcb = xc.astype(BF16)
        tr = _half_tanh_half(_dot(xcb, wr_ref[nb]) + br_ref[:, sl])
        ti = _half_tanh_half(_dot(xcb, wi_ref[nb]) + bi_ref[:, sl])
        log_a = neg_half_c_sp[:, sl] * (1.0 + tr)
        th = jnp.tanh(log_a)
        m2 = -2.0 * th / (1.0 - th)
        mult = jnp.where(m2 > 0.0, m2 * lax.rsqrt(m2), 0.0)
        if first_pos_zero:
            mult = jnp.where(jnp.logical_and(first_row, t == 0), 1.0, mult)
        a_ref[:, sl] = jnp.exp(log_a)
        b_ref[:, sl] = mult * ((0.5 * xc) * (1.0 + ti))

    if interleaved:
        def tile_scan(i, c):
            hloc, prod = c
            rows = pl.ds(pl.multiple_of(i * SUBLANES, SUBLANES), SUBLANES)
            a = a_ref[rows, :]
            hloc = a * hloc + b_ref[rows, :]
            prod = a * prod
            b_ref[rows, :] = hloc
            a_ref[rows, :] = prod
            return hloc, prod

        init = (jnp.zeros((SUBLANES, RG_WIDTH), F32), jnp.ones((SUBLANES, RG_WIDTH), F32))
        hloc, prod = lax.fori_loop(0, seg, tile_scan, init, unroll=4)
        carry = h_ref[...]
        for s in range(SUBLANES):
            cin_ref[s:s + 1, :] = carry
            carry = hloc[s:s + 1, :] + prod[s:s + 1, :] * carry
        h_ref[...] = carry
        cin = cin_ref[...]

        def tile_out(i, _):
            rows = pl.ds(pl.multiple_of(i * SUBLANES, SUBLANES), SUBLANES)
            h = b_ref[rows, :] + a_ref[rows, :] * cin
            b_ref[rows, :] = h * gate_ref[rows, :]
            return 0

        lax.fori_loop(0, seg, tile_out, 0, unroll=4)
        o_ref[0] = _dot(perm_ref[1], b_ref[...].astype(BF16)).astype(o_ref.dtype)
    else:
        def step_scan(i, h):
            h = a_ref[pl.ds(i, 1), :] * h + b_ref[pl.ds(i, 1), :]
            b_ref[pl.ds(i, 1), :] = h
            return h

        h_ref[...] = lax.fori_loop(0, tb, step_scan, h_ref[...], unroll=8)
        o_ref[0] = (b_ref[...] * gate_ref[...]).astype(o_ref.dtype)

    @pl.when(t == pl.num_programs(1) - 1)
    def _():
        h_out_ref[0] = h_ref[...]
        buf_out_ref[0] = tail_ref[...]


def _rglru(hn, w, cw, cb, wr, br, wi, bi, lam, h0, buf0, tb, pos0):
    b, t, _ = hn.shape
    nt = t // tb
    interleaved = tb % LANES == 0
    vec = lambda a: a.reshape(1, RG_WIDTH)
    return pl.pallas_call(
        functools.partial(_rg_kernel, tb=tb, first_pos_zero=(pos0 == 0), interleaved=interleaved),
        grid=(b, nt),
        in_specs=[
            pl.BlockSpec((1, tb, D_MODEL), lambda i, j: (i, j, 0)),
            _resident((D_MODEL, RG_COLS)),
            _resident((RG_CONV, RG_WIDTH)),
            _resident((1, RG_WIDTH)),
            _resident((RG_BLOCKS, RG_BLOCK, RG_BLOCK)),
            _resident((1, RG_WIDTH)),
            _resident((RG_BLOCKS, RG_BLOCK, RG_BLOCK)),
            _resident((1, RG_WIDTH)),
            _resident((1, RG_WIDTH)),
            pl.BlockSpec((1, 1, RG_WIDTH), lambda i, j: (i, 0, 0)),
            pl.BlockSpec((1, RG_CONV - 1, RG_WIDTH), lambda i, j: (i, 0, 0)),
        ],
        out_specs=[
            pl.BlockSpec((1, tb, RG_WIDTH), lambda i, j: (i, j, 0)),
            pl.BlockSpec((1, 1, RG_WIDTH), lambda i, j: (i, 0, 0)),
            pl.BlockSpec((1, RG_CONV - 1, RG_WIDTH), lambda i, j: (i, 0, 0)),
        ],
        out_shape=[
            jax.ShapeDtypeStruct((b, t, RG_WIDTH), BF16),
            jax.ShapeDtypeStruct((b, 1, RG_WIDTH), F32),
            jax.ShapeDtypeStruct((b, RG_CONV - 1, RG_WIDTH), F32),
        ],
        scratch_shapes=[
            pltpu.VMEM((tb, RG_WIDTH), F32),
            pltpu.VMEM((tb + (RG_CONV - 1) * SUBLANES, RG_WIDTH), F32),
            pltpu.VMEM((tb, RG_WIDTH), F32),
            pltpu.VMEM((tb, RG_WIDTH), F32),
            pltpu.VMEM((2, tb, tb), BF16),
            pltpu.VMEM((RG_CONV - 1, RG_WIDTH), F32),
            pltpu.VMEM((SUBLANES, RG_WIDTH), F32),
            pltpu.VMEM((1, RG_WIDTH), F32),
        ],
        compiler_params=_cparams(("parallel", "arbitrary")),
        name="rglru",
    )(hn, w, cw, vec(cb), wr, vec(br), wi, vec(bi), vec(lam), h0.reshape(b, 1, RG_WIDTH), buf0)


def _merge_kernel(hn_ref, oret_ref, ohg_ref, org_ref, x_ref, wg_ref, wb_ref, wo_ref, out_ref):
    hn = hn_ref[...]
    mixed = _sigmoid(_dot(hn, wg_ref[:, 0:D_MODEL])) * _dot(oret_ref[...], wb_ref[0:RET_V, :])
    mixed = mixed + _sigmoid(_dot(hn, wg_ref[:, D_MODEL:2 * D_MODEL])) * _dot(
        ohg_ref[...], wb_ref[RET_V:RET_V + HG_V, :])
    mixed = mixed + _sigmoid(_dot(hn, wg_ref[:, 2 * D_MODEL:3 * D_MODEL])) * _dot(
        org_ref[...], wb_ref[RET_V + HG_V:RET_V + HG_V + RG_WIDTH, :])
    out_ref[...] = x_ref[...] + _dot(mixed.astype(BF16), wo_ref[...])


def _merge(hn, o_ret, o_hg, o_rg, x, wg, wb, wo, tm):
    rows = x.shape[0]
    rowspec = lambda width: pl.BlockSpec((tm, width), lambda i: (i, 0))
    return pl.pallas_call(
        _merge_kernel,
        grid=(rows // tm,),
        in_specs=[rowspec(D_MODEL), rowspec(RET_V), rowspec(HG_V), rowspec(RG_WIDTH), rowspec(D_MODEL),
                  _resident((D_MODEL, GATE_COLS)), _resident((RET_V + HG_V + RG_WIDTH, D_MODEL)),
                  _resident((D_MODEL, D_MODEL))],
        out_specs=rowspec(D_MODEL),
        out_shape=jax.ShapeDtypeStruct((rows, D_MODEL), F32),
        compiler_params=_cparams(("parallel",)),
        name="merge",
    )(hn, o_ret, o_hg, o_rg, x, wg, wb, wo)


def _ffn_kernel(x_ref, n2_ref, wup_ref, cw_ref, cb_ref, wdn_ref, buf0_ref, fn_ref, out_ref, buf_out_ref,
                abuf_ref, *, tb, final_norm):
    t = pl.program_id(1)
    hist = SUBLANES

    @pl.when(t == 0)
    def _():
        abuf_ref[0:hist, :] = jnp.zeros((hist, D_FF), F32)
        abuf_ref[hist - (FFN_CONV - 1):hist, :] = buf0_ref[0]

    x = x_ref[0]
    hn = _rms(x, n2_ref[...]).astype(BF16)
    abuf_ref[hist:hist + tb, :] = _dot(hn, wup_ref[:, 0:D_FF])
    gate = _dot(hn, wup_ref[:, D_FF:2 * D_FF])
    a = cb_ref[...]
    for j in range(FFN_CONV):
        a = a + cw_ref[j:j + 1, :] * abuf_ref[hist - (FFN_CONV - 1) + j:hist - (FFN_CONV - 1) + j + tb, :]
    y = x + _dot((_gelu(a) * gate).astype(BF16), wdn_ref[...])
    if final_norm:
        y = _rms(y, fn_ref[...])
    out_ref[0] = y

    @pl.when(t == pl.num_programs(1) - 1)
    def _():
        buf_out_ref[0] = abuf_ref[tb + hist - (FFN_CONV - 1):tb + hist, :]

    abuf_ref[0:hist, :] = abuf_ref[tb:tb + hist, :]


def _convffn(x, n2, wup, cw, cb, wdn, buf0, fn, tb, final_norm):
    b, t, _ = x.shape
    nt = t // tb
    return pl.pallas_call(
        functools.partial(_ffn_kernel, tb=tb, final_norm=final_norm),
        grid=(b, nt),
        in_specs=[
            pl.BlockSpec((1, tb, D_MODEL), lambda i, j: (i, j, 0)),
            _resident((1, D_MODEL)),
            _resident((D_MODEL, 2 * D_FF)),
            _resident((FFN_CONV, D_FF)),
            _resident((1, D_FF)),
            _resident((D_FF, D_MODEL)),
            pl.BlockSpec((1, FFN_CONV - 1, D_FF), lambda i, j: (i, 0, 0)),
            _resident((1, D_MODEL)),
        ],
        out_specs=[
            pl.BlockSpec((1, tb, D_MODEL), lambda i, j: (i, j, 0)),
            pl.BlockSpec((1, FFN_CONV - 1, D_FF), lambda i, j: (i, 0, 0)),
        ],
        out_shape=[
            jax.ShapeDtypeStruct((b, t, D_MODEL), F32),
            jax.ShapeDtypeStruct((b, FFN_CONV - 1, D_FF), F32),
        ],
        scratch_shapes=[pltpu.VMEM((tb + SUBLANES, D_FF), F32)],
        compiler_params=_cparams(("parallel", "arbitrary")),
        name="convffn",
    )(x, n2.reshape(1, D_MODEL), wup, cw, cb.reshape(1, D_FF), wdn, buf0, fn.reshape(1, D_MODEL))


def _rope_tables(t, pos0):
    half = RET_DK // 2
    inv = jnp.power(ROPE_BASE, -jnp.arange(half, dtype=F32) / half)
    ang = (jnp.arange(t, dtype=F32) + pos0)[:, None] * inv[None, :]
    cos, sin = jnp.cos(ang), jnp.sin(ang)
    return jnp.concatenate([cos, cos], axis=-1), jnp.concatenate([-sin, sin], axis=-1)


def _block_sizes(t):
    if t <= CHUNK:
        return t, t, t
    return TIME_BLOCK, WIDE_TIME_BLOCK, CHUNK


def _trunk(x, pos0, r0, s0, h0, rgb0, ffb0, p):
    b, t, _ = x.shape
    tb, tb_wide, chunk = _block_sizes(t)
    rows = b * t
    tm = min(rows, ROW_BLOCK)
    cos2, sin2 = _rope_tables(t, pos0)
    depth = p["norm1_w"].shape[0]
    states = []
    x2d = x.reshape(rows, D_MODEL)
    for l in range(depth):
        o_ret, r_new, hn = _retention(
            x2d.reshape(b, t, D_MODEL), p["norm1_w"][l], p["w_ret"][l], cos2, sin2, r0[l], tb, chunk)
        hn2d = hn.reshape(rows, D_MODEL)
        o_hg, s_new = _hgrn(hn, p["w_hg"][l], p["hg_lb"], p["hg_norm_w"][l], s0[l], tb_wide, chunk, l)
        o_rg, h_new, rgb_new = _rglru(
            hn, p["w_rg"][l], p["rg_conv_w"][l], p["rg_conv_b"][l], p["rg_w_r"][l], p["rg_b_r"][l],
            p["rg_w_i"][l], p["rg_b_i"][l], p["rg_lambda"][l], h0[l], rgb0[l], tb, pos0)
        x2d = _merge(hn2d, o_ret.reshape(rows, RET_V), o_hg.reshape(rows, HG_V), o_rg.reshape(rows, RG_WIDTH),
                     x2d, p["w_gate"][l], p["w_branch"][l], p["w_out"][l], tm)
        x3d, ffb_new = _convffn(
            x2d.reshape(b, t, D_MODEL), p["norm2_w"][l], p["w_up"][l], p["ffn_conv_w"][l], p["ffn_conv_b"][l],
            p["w_down"][l], ffb0[l], p["final_norm_w"], tb_wide, l == depth - 1)
        x2d = x3d.reshape(rows, D_MODEL)
        states.append((r_new, s_new, h_new.reshape(b, RG_WIDTH), rgb_new, ffb_new))
    new_states = tuple(jnp.stack(st, axis=0) for st in zip(*states))
    return x2d.reshape(b, t, D_MODEL), new_states


def kernel(x_prompt, x_sample, state_ret, state_hgrn, state_rglru, cache_rg_conv, cache_ffn_conv,
           norm1_w, w_in, w_branch, w_out, rg_conv_w, rg_conv_b, rg_w_r, rg_b_r, rg_w_i, rg_b_i,
           rg_lambda, hg_lb, hg_norm_w, norm2_w, w_up, ffn_conv_w, ffn_conv_b, w_down, final_norm_w):
    depth = w_in.shape[0]
    batch = x_prompt.shape[0]
    bf = lambda a: a.astype(BF16)
    p = dict(
        norm1_w=norm1_w, norm2_w=norm2_w, final_norm_w=final_norm_w, hg_lb=hg_lb, hg_norm_w=hg_norm_w,
        w_ret=bf(w_in[:, :, 0:OFF_HG]), w_hg=bf(w_in[:, :, OFF_HG:OFF_RG]),
        w_rg=bf(w_in[:, :, OFF_RG:OFF_GATE]), w_gate=bf(w_in[:, :, OFF_GATE:OFF_GATE + GATE_COLS]),
        w_branch=bf(w_branch), w_out=bf(w_out), rg_conv_w=rg_conv_w, rg_conv_b=rg_conv_b,
        rg_w_r=bf(rg_w_r), rg_b_r=rg_b_r, rg_w_i=bf(rg_w_i), rg_b_i=rg_b_i, rg_lambda=rg_lambda,
        w_up=bf(w_up), ffn_conv_w=ffn_conv_w, ffn_conv_b=ffn_conv_b, w_down=bf(w_down),
    )
    zeros = lambda *shape: jnp.zeros((depth, batch) + shape, F32)
    y_p, (ret_p, hg_p, rgh_p, rgc_p, ffc_p) = _trunk(
        x_prompt, 0, zeros(RET_HEADS, RET_DK, RET_DV), zeros(HG_HEADS, HG_DK, HG_DV), zeros(RG_WIDTH),
        zeros(RG_CONV - 1, RG_WIDTH), zeros(FFN_CONV - 1, D_FF), p)
    y_s, (ret_s, hg_s, rgh_s, rgc_s, ffc_s) = _trunk(
        x_sample, PAST_LEN, state_ret, state_hgrn, state_rglru, cache_rg_conv, cache_ffn_conv, p)
    return (y_p, y_s, ret_p, ret_s, hg_p, hg_s, rgh_p, rgh_s, rgc_p, rgc_s, ffc_p, ffc_s)
```
